```python
import jax, jax.numpy as jnp
from jax import lax
import numpy as np

D_MODEL = 2048
BATCH = 8
SEQ = 2048
DEPTH = 2

GRID_W = 64
CTX_LEN = 256
HEAD_DIM = 128
N_Q_HEADS = D_MODEL // (2 * HEAD_DIM)
N_KV_HEADS = 2
GQA_GROUP = N_Q_HEADS // N_KV_HEADS
ATTN_WIDTH = N_Q_HEADS * HEAD_DIM
KV_WIDTH = N_KV_HEADS * HEAD_DIM
WINDOW = 128
BLOCK = 128
ROPE_BASE = 10000.0
ROT_AXIS_DIM = HEAD_DIM // 2
POOL_WINDOWS = (2, 4, 8, 16)
N_POOL_GROUPS = len(POOL_WINDOWS)
POOL_WIDTH = D_MODEL // 4
POOL_GROUP_DIM = POOL_WIDTH // N_POOL_GROUPS
SGU_WIDTH = D_MODEL // 4
N_SGU_HEADS = 4
SGU_HEAD_DIM = SGU_WIDTH // N_SGU_HEADS
SGU_CHUNK = 128
MIX_WIDTH = ATTN_WIDTH + POOL_WIDTH + SGU_WIDTH
Q_END = ATTN_WIDTH
K_END = Q_END + KV_WIDTH
V_END = K_END + KV_WIDTH
P_END = V_END + POOL_WIDTH
U_END = P_END + SGU_WIDTH
IN_WIDTH = U_END + SGU_WIDTH
D_FF = ((8 * D_MODEL // 3 + 255) // 256) * 256
CONV_WIDTH = 3
N_MOD = 6
EPS = 1e-6

kernel_name = "hybrid_parallel_heads_dit_block"


def rms_norm(x, g):
    xf = x.astype(jnp.float32)
    y = xf * lax.rsqrt(jnp.mean(xf * xf, axis=-1, keepdims=True) + EPS)
    return (y * g.astype(jnp.float32)).astype(x.dtype)


def modulate(h, shift, scale):
    return h * (1 + scale) + shift


def axial_rope_tables(L):
    rows = L // GRID_W
    row_ids = jnp.repeat(jnp.arange(rows), GRID_W).astype(jnp.float32)
    col_ids = jnp.tile(jnp.arange(GRID_W), rows).astype(jnp.float32)
    inv = 1.0 / (ROPE_BASE ** (jnp.arange(0, ROT_AXIS_DIM, 2, dtype=jnp.float32) / ROT_AXIS_DIM))
    ang_r = row_ids[:, None] * inv
    ang_c = col_ids[:, None] * inv
    return jnp.cos(ang_r), jnp.sin(ang_r), jnp.cos(ang_c), jnp.sin(ang_c)


def rotate_half_pairs(x, cos, sin):
    half = x.shape[-1] // 2
    x1, x2 = x[..., :half], x[..., half:]
    cos = cos[None, :, None, :]
    sin = sin[None, :, None, :]
    return jnp.concatenate([x1 * cos - x2 * sin, x2 * cos + x1 * sin], axis=-1)


def apply_axial_rope(x, tables):
    cr, sr, cc, sc = tables
    xf = x.astype(jnp.float32)
    yr = rotate_half_pairs(xf[..., :ROT_AXIS_DIM], cr, sr)
    yc = rotate_half_pairs(xf[..., ROT_AXIS_DIM:], cc, sc)
    return jnp.concatenate([yr, yc], axis=-1).astype(x.dtype)


def split_proj(p):
    return jnp.split(p, [Q_END, K_END, V_END, P_END, U_END], axis=-1)


def latent_window_attention(q, k, v, kc, vc, sink):
    B, L = q.shape[0], q.shape[1]
    nb = L // BLOCK
    scale = HEAD_DIM ** -0.5
    qb = q.reshape(B, nb, BLOCK, N_KV_HEADS, GQA_GROUP, HEAD_DIM)
    pad = ((0, 0), (1, 1), (0, 0), (0, 0), (0, 0))
    kp = jnp.pad(k.reshape(B, nb, BLOCK, N_KV_HEADS, HEAD_DIM), pad)
    vp = jnp.pad(v.reshape(B, nb, BLOCK, N_KV_HEADS, HEAD_DIM), pad)
    kband = jnp.concatenate([kp[:, :-2], kp[:, 1:-1], kp[:, 2:]], axis=2)
    vband = jnp.concatenate([vp[:, :-2], vp[:, 1:-1], vp[:, 2:]], axis=2)
    s_band = jnp.einsum('bnqkgd,bnskd->bnkgqs', qb, kband).astype(jnp.float32) * scale
    qi = jnp.arange(nb)[:, None] * BLOCK + jnp.arange(BLOCK)[None, :]
    kj = (jnp.arange(nb)[:, None] - 1) * BLOCK + jnp.arange(3 * BLOCK)[None, :]
    valid = ((kj[:, None, :] >= 0) & (kj[:, None, :] < L)
             & (jnp.abs(qi[:, :, None] - kj[:, None, :]) <= WINDOW))
    s_band = jnp.where(valid[None, :, None, None], s_band, -jnp.inf)
    s_ctx = jnp.einsum('bnqkgd,bskd->bnkgqs', qb, kc).astype(jnp.float32) * scale
    s_sink = jnp.broadcast_to(
        sink.astype(jnp.float32).reshape(N_KV_HEADS, GQA_GROUP)[None, None, :, :, None, None],
        s_band.shape[:-1] + (1,))
    p = jax.nn.softmax(jnp.concatenate([s_band, s_ctx, s_sink], axis=-1), axis=-1)
    nband = 3 * BLOCK
    nctx = kc.shape[1]
    p_band = p[..., :nband].astype(v.dtype)
    p_ctx = p[..., nband:nband + nctx].astype(v.dtype)
    o = (jnp.einsum('bnkgqs,bnskd->bnqkgd', p_band, vband)
         + jnp.einsum('bnkgqs,bskd->bnqkgd', p_ctx, vc))
    return o.reshape(B, L, ATTN_WIDTH)


def context_attention(qc, kc, vc, sink):
    B, Lc = qc.shape[0], qc.shape[1]
    qg = qc.reshape(B, Lc, N_KV_HEADS, GQA_GROUP, HEAD_DIM)
    s = jnp.einsum('bqkgd,bskd->bkgqs', qg, kc).astype(jnp.float32) * (HEAD_DIM ** -0.5)
    s_sink = jnp.broadcast_to(
        sink.astype(jnp.float32).reshape(N_KV_HEADS, GQA_GROUP)[None, :, :, None, None],
        s.shape[:-1] + (1,))
    p = jax.nn.softmax(jnp.concatenate([s, s_sink], axis=-1), axis=-1)[..., :-1]
    o = jnp.einsum('bkgqs,bskd->bqkgd', p.astype(vc.dtype), vc)
    return o.reshape(B, Lc, ATTN_WIDTH)


def multiscale_pool(pv, pool_w, pool_scale):
    B, L, _ = pv.shape
    xf = pv.astype(jnp.float32).reshape(B, L, N_POOL_GROUPS, POOL_GROUP_DIM)
    cs = jnp.pad(jnp.cumsum(xf, axis=1), ((0, 0), (1, 0), (0, 0), (0, 0)))
    t = jnp.arange(L)
    means = []
    for gi, w in enumerate(POOL_WINDOWS):
        lo = jnp.clip(t - w // 2, 0, L)
        hi = jnp.clip(t - w // 2 + w, 0, L)
        s = cs[:, hi, gi] - cs[:, lo, gi]
        means.append(s / (hi - lo).astype(jnp.float32)[None, :, None])
    pooled = jnp.stack(means, axis=2)
    y = (pooled - xf).astype(pv.dtype)
    y = jnp.einsum('blgc,gcd->blgd', y, pool_w)
    return y.reshape(B, L, POOL_WIDTH) * pool_scale


def spatial_gating(u, v, g_norm, w_s, b_s):
    B, L, _ = u.shape
    u = jax.nn.gelu(u, approximate=False)
    v = rms_norm(jax.nn.gelu(v, approximate=False), g_norm)
    vch = v.reshape(B, L // SGU_CHUNK, SGU_CHUNK, N_SGU_HEADS, SGU_HEAD_DIM)
    vs = jnp.einsum('hpq,bnqhc->bnphc', w_s, vch) + b_s.T[:, :, None]
    return u * vs.reshape(B, L, SGU_WIDTH)


def conv_ffn(h, w_up, conv_w, conv_b, w_down):
    L = h.shape[1]
    a = h @ w_up
    ap = jnp.pad(a, ((0, 0), (1, 1), (0, 0)))
    a = conv_b + ap[:, 0:L] * conv_w[0] + ap[:, 1:L + 1] * conv_w[1] + ap[:, 2:L + 2] * conv_w[2]
    gate, val = jnp.split(a, 2, axis=-1)
    return (jax.nn.silu(gate) * val) @ w_down


def setup_inputs(seed: int = 0) -> dict:
    key = jax.random.key(seed)
    ks = jax.random.split(key, 24)
    D = D_MODEL

    def nrm(k, shape, s):
        return jax.random.normal(k, shape, jnp.float32) * s

    return {
        "x": nrm(ks[0], (BATCH, SEQ, D), 1.0),
        "c": nrm(ks[1], (BATCH, D), 1.0),
        "ctx": nrm(ks[2], (BATCH, CTX_LEN, D), 1.0),
        "c_ctx": nrm(ks[3], (D,), 1.0),
        "norm1_g": 1.0 + nrm(ks[4], (DEPTH, D), 0.02),
        "norm2_g": 1.0 + nrm(ks[5], (DEPTH, D), 0.02),
        "w_ada": nrm(ks[6], (DEPTH, D, N_MOD * D), D ** -0.5),
        "b_ada": nrm(ks[7], (DEPTH, N_MOD * D), 0.02),
        "w_in": nrm(ks[8], (DEPTH, D, IN_WIDTH), D ** -0.5),
        "q_norm_g": 1.0 + nrm(ks[9], (DEPTH, HEAD_DIM), 0.02),
        "k_norm_g": 1.0 + nrm(ks[10], (DEPTH, HEAD_DIM), 0.02),
        "attn_sink": nrm(ks[11], (DEPTH, N_Q_HEADS), 0.5),
        "pool_w": nrm(ks[12], (DEPTH, N_POOL_GROUPS, POOL_GROUP_DIM, POOL_GROUP_DIM), POOL_GROUP_DIM ** -0.5),
        "pool_scale": 1.0 + nrm(ks[13], (DEPTH, POOL_WIDTH), 0.02),
        "sgu_norm_g": 1.0 + nrm(ks[14], (DEPTH, SGU_WIDTH), 0.02),
        "sgu_w": nrm(ks[15], (DEPTH, N_SGU_HEADS, SGU_CHUNK, SGU_CHUNK), SGU_CHUNK ** -0.5),
        "sgu_b": 1.0 + nrm(ks[16], (DEPTH, N_SGU_HEADS, SGU_CHUNK), 0.02),
        "w_out": nrm(ks[17], (DEPTH, MIX_WIDTH, D), MIX_WIDTH ** -0.5),
        "w_up": nrm(ks[18], (DEPTH, D, 2 * D_FF), D ** -0.5),
        "conv_w": nrm(ks[19], (DEPTH, CONV_WIDTH, 2 * D_FF), CONV_WIDTH ** -0.5),
        "conv_b": nrm(ks[20], (DEPTH, 2 * D_FF), 0.02),
        "w_down": nrm(ks[21], (DEPTH, D_FF, D), D_FF ** -0.5),
        "final_norm_g": 1.0 + nrm(ks[22], (D,), 0.02),
    }


def reference(x, c, ctx, c_ctx, norm1_g, norm2_g, w_ada, b_ada, w_in, q_norm_g, k_norm_g,
              attn_sink, pool_w, pool_scale, sgu_norm_g, sgu_w, sgu_b, w_out, w_up, conv_w,
              conv_b, w_down, final_norm_g):
    B, L, D = x.shape
    Lc = ctx.shape[1]
    rope = axial_rope_tables(L)
    silu_c = jax.nn.silu(c)
    silu_cc = jax.nn.silu(c_ctx)
    xc = ctx
    for l in range(DEPTH):
        last = l == DEPTH - 1
        mod = (silu_c @ w_ada[l] + b_ada[l]).reshape(B, N_MOD, 1, D)
        mod_c = (silu_cc @ w_ada[l] + b_ada[l]).reshape(N_MOD, D)

        h = modulate(rms_norm(x, norm1_g[l]), mod[:, 0], mod[:, 1])
        hc = modulate(rms_norm(xc, norm1_g[l]), mod_c[0], mod_c[1])
        q, k, v, pv, u, g = split_proj(h @ w_in[l])
        qc, kc, vc, pvc, uc, gc = split_proj(hc @ w_in[l])

        q = apply_axial_rope(rms_norm(q.reshape(B, L, N_Q_HEADS, HEAD_DIM), q_norm_g[l]), rope)
        k = apply_axial_rope(rms_norm(k.reshape(B, L, N_KV_HEADS, HEAD_DIM), k_norm_g[l]), rope)
        v = v.reshape(B, L, N_KV_HEADS, HEAD_DIM)
        kc = rms_norm(kc.reshape(B, Lc, N_KV_HEADS, HEAD_DIM), k_norm_g[l])
        vc = vc.reshape(B, Lc, N_KV_HEADS, HEAD_DIM)

        attn = latent_window_attention(q, k, v, kc, vc, attn_sink[l])
        pool = multiscale_pool(pv, pool_w[l], pool_scale[l])
        sgu = spatial_gating(u, g, sgu_norm_g[l], sgu_w[l], sgu_b[l])
        mix = jnp.concatenate([attn, pool, sgu], axis=-1) @ w_out[l]
        x = x + mod[:, 2] * mix

        if not last:
            qc = rms_norm(qc.reshape(B, Lc, N_Q_HEADS, HEAD_DIM), q_norm_g[l])
            attn_c = context_attention(qc, kc, vc, attn_sink[l])
            pool_c = multiscale_pool(pvc, pool_w[l], pool_scale[l])
            sgu_c = spatial_gating(uc, gc, sgu_norm_g[l], sgu_w[l], sgu_b[l])
            mix_c = jnp.concatenate([attn_c, pool_c, sgu_c], axis=-1) @ w_out[l]
            xc = xc + mod_c[2] * mix_c

        hf = modulate(rms_norm(x, norm2_g[l]), mod[:, 3], mod[:, 4])
        x = x + mod[:, 5] * conv_ffn(hf, w_up[l], conv_w[l], conv_b[l], w_down[l])
        if not last:
            hfc = modulate(rms_norm(xc, norm2_g[l]), mod_c[3], mod_c[4])
            xc = xc + mod_c[5] * conv_ffn(hfc, w_up[l], conv_w[l], conv_b[l], w_down[l])

    return rms_norm(x, final_norm_g)
```

```python
import functools

import jax
import jax.numpy as jnp
from jax import lax
from jax.experimental import pallas as pl
from jax.experimental.pallas import tpu as pltpu

F32 = jnp.float32
BF16 = jnp.bfloat16

D_MODEL = 2048
GRID_W = 64
HEAD_DIM = 128
N_Q_HEADS = 8
N_KV_HEADS = 2
GQA_GROUP = N_Q_HEADS // N_KV_HEADS
ATTN_WIDTH = N_Q_HEADS * HEAD_DIM
WINDOW = 128
BLOCK = 128
ROPE_BASE = 10000.0
ROT_AXIS_DIM = HEAD_DIM // 2
POOL_WINDOWS = (2, 4, 8, 16)
POOL_WIDTH = 512
POOL_GROUP_DIM = 128
SGU_WIDTH = 512
N_SGU_HEADS = 4
SGU_CHUNK = 128
IN_WIDTH = 3072
D_FF = 5632
N_MOD = 6
EPS = 1e-6

SUBLANES = 8
BF16_SUBLANES = 16
LANES = 128
HALO = BF16_SUBLANES
MOD_ROWS = 16
IN_TILE = 512
FF_TILE = 512
VMEM_LIMIT = 56 * 1024 * 1024


def _cparams(n_axes):
    return pltpu.CompilerParams(
        dimension_semantics=("arbitrary",) * n_axes, vmem_limit_bytes=VMEM_LIMIT)


def _rms(x, g):
    return x * lax.rsqrt(jnp.mean(x * x, axis=-1, keepdims=True) + EPS) * g


def _gelu(x):
    return 0.5 * x * (1.0 + lax.erf(x * 0.7071067811865476))


def _silu(x):
    return x * jax.nn.sigmoid(x)


def _row_chunks(n_rows, chunk, fn):
    chunk = min(chunk, n_rows)
    assert n_rows % chunk == 0

    def body(i, carry):
        fn(pl.multiple_of(i * chunk, chunk))
        return carry

    lax.fori_loop(0, n_rows // chunk, body, 0)


def _mod_kernel(c_ref, w_ref, b_ref, o_ref):
    c = c_ref[...]
    s = _silu(c).astype(BF16)
    o_ref[...] = jnp.dot(s, w_ref[...].astype(BF16), preferred_element_type=F32) + b_ref[...]


def _modulation(cc, w_ada, b_ada):
    depth, d, n = w_ada.shape
    tn = 1024
    return pl.pallas_call(
        _mod_kernel,
        grid=(depth, n // tn),
        in_specs=[
            pl.BlockSpec((MOD_ROWS, d), lambda l, j: (0, 0)),
            pl.BlockSpec((None, d, tn), lambda l, j: (l, 0, j)),
            pl.BlockSpec((None, 1, tn), lambda l, j: (l, 0, j)),
        ],
        out_specs=pl.BlockSpec((None, MOD_ROWS, tn), lambda l, j: (l, 0, j)),
        out_shape=jax.ShapeDtypeStruct((depth, MOD_ROWS, n), F32),
        compiler_params=_cparams(2),
        name="modulation",
    )(cc, w_ada, b_ada.reshape(depth, 1, n))


def _rope(x, cos, sin_signed):
    lane = lax.broadcasted_iota(jnp.int32, x.shape, 1)
    first = (lane % ROT_AXIS_DIM) < (ROT_AXIS_DIM // 2)
    partner = jnp.where(first, pltpu.roll(x, HEAD_DIM - ROT_AXIS_DIM // 2, 1),
                        pltpu.roll(x, ROT_AXIS_DIM // 2, 1))
    return x * cos + partner * sin_signed


def _in_kernel(*refs, tm, tiles, rope):
    n_in = 9 if rope else 7
    x_ref, mod_ref, g1_ref, w_ref, qg_ref, kg_ref, sg_ref = refs[:7]
    cos_ref, sin_ref = refs[7:9] if rope else (None, None)
    out_refs = refs[n_in:-2]
    hs_ref, acc_ref = refs[-2:]
    outs = {}
    names = [nm for nm, ids in (("q", (0, 1)), ("kv", (2,)), ("pv", (3,)), ("u", (4,)), ("v", (5,)))
             if any(t in tiles for t in ids)]
    for nm, r in zip(names, out_refs):
        outs[nm] = r
    n = pl.program_id(1)
    chunk = 256

    @pl.when(n == 0)
    def _():
        shift = mod_ref[0:1, :]
        scale1 = 1.0 + mod_ref[1:2, :]
        g1 = g1_ref[...]

        def body(r):
            h = _rms(x_ref[pl.ds(r, chunk), :], g1) * scale1 + shift
            hs_ref[pl.ds(r, chunk), :] = h.astype(BF16)

        _row_chunks(tm, chunk, body)

    acc_ref[...] = jnp.dot(hs_ref[...], w_ref[...], preferred_element_type=F32)

    def head(r, h, g):
        xh = _rms(acc_ref[pl.ds(r, chunk), h * HEAD_DIM:(h + 1) * HEAD_DIM], g)
        if rope:
            xh = _rope(xh, cos_ref[pl.ds(r, chunk), :], sin_ref[pl.ds(r, chunk), :])
        return xh

    for idx, t in enumerate(tiles):
        if t in (0, 1):
            def q_tile(t=t):
                qg = qg_ref[...]

                def body(r):
                    for h in range(IN_TILE // HEAD_DIM):
                        xh = head(r, h, qg) * (HEAD_DIM ** -0.5)
                        c0 = t * IN_TILE + h * HEAD_DIM
                        outs["q"][pl.ds(r, chunk), c0:c0 + HEAD_DIM] = xh.astype(BF16)

                _row_chunks(tm, chunk, body)
            pl.when(n == idx)(q_tile)
        elif t == 2:
            def kv_tile():
                kg = kg_ref[...]

                def body(r):
                    for h in range(N_KV_HEADS):
                        c0 = h * HEAD_DIM
                        outs["kv"][pl.ds(r, chunk), c0:c0 + HEAD_DIM] = head(r, h, kg).astype(BF16)
                    half = N_KV_HEADS * HEAD_DIM
                    outs["kv"][pl.ds(r, chunk), half:] = acc_ref[pl.ds(r, chunk), half:].astype(BF16)

                _row_chunks(tm, chunk, body)
            pl.when(n == idx)(kv_tile)
        elif t == 3:
            def pv_tile():
                outs["pv"][...] = acc_ref[...]
            pl.when(n == idx)(pv_tile)
        elif t == 4:
            def u_tile():
                def body(r):
                    outs["u"][pl.ds(r, chunk), :] = _gelu(acc_ref[pl.ds(r, chunk), :])

                _row_chunks(tm, chunk, body)
            pl.when(n == idx)(u_tile)
        else:
            def v_tile():
                sg = sg_ref[...]

                def body(r):
                    outs["v"][pl.ds(r, chunk), :] = _rms(
                        _gelu(acc_ref[pl.ds(r, chunk), :]), sg).astype(BF16)

                _row_chunks(tm, chunk, body)
            pl.when(n == idx)(v_tile)


def _in_proj(x2, mod4, layer, mod_row, g1, w_in, qg, kg, sg, rope_tabs, *, seq, tm, tiles):
    rows, d = x2.shape
    assert rows % tm == 0 and seq % tm == 0
    tiles = tuple(tiles)
    assert tiles == tuple(range(tiles[0], tiles[0] + len(tiles)))
    t0 = tiles[0]
    rope = rope_tabs is not None
    per_seq = seq // tm

    def mrow(m):
        return mod_row if mod_row is not None else m // per_seq

    in_specs = [
        pl.BlockSpec((tm, d), lambda m, n: (m, 0)),
        pl.BlockSpec((None, None, N_MOD, d), lambda m, n: (layer, mrow(m), 0, 0)),
        pl.BlockSpec((1, d), lambda m, n: (0, 0)),
        pl.BlockSpec((d, IN_TILE), lambda m, n: (0, n + t0)),
        pl.BlockSpec((1, HEAD_DIM), lambda m, n: (0, 0)),
        pl.BlockSpec((1, HEAD_DIM), lambda m, n: (0, 0)),
        pl.BlockSpec((1, SGU_WIDTH), lambda m, n: (0, 0)),
    ]
    args = [x2, mod4, g1, w_in, qg, kg, sg]
    if rope:
        in_specs += [pl.BlockSpec((tm, HEAD_DIM), lambda m, n: (m % per_seq, 0))] * 2
        args += list(rope_tabs)
    out_specs, out_shape = [], []

    def add(width, dtype):
        out_specs.append(pl.BlockSpec((tm, width), lambda m, n: (m, 0)))
        out_shape.append(jax.ShapeDtypeStruct((rows, width), dtype))

    if 0 in tiles or 1 in tiles:
        assert 0 in tiles and 1 in tiles
        add(ATTN_WIDTH, BF16)
    if 2 in tiles:
        add(IN_TILE, BF16)
    if 3 in tiles:
        add(POOL_WIDTH, F32)
    if 4 in tiles:
        add(SGU_WIDTH, F32)
    if 5 in tiles:
        add(SGU_WIDTH, BF16)
    return pl.pallas_call(
        functools.partial(_in_kernel, tm=tm, tiles=tiles, rope=rope),
        grid=(rows // tm, len(tiles)),
        in_specs=in_specs,
        out_specs=out_specs,
        out_shape=out_shape,
        scratch_shapes=[pltpu.VMEM((tm, d), BF16), pltpu.VMEM((tm, IN_TILE), F32)],
        compiler_params=_cparams(2),
        name="in_proj",
    )(*args)


def _attn_kernel(*refs, seq, band):
    if band:
        sink_ref, q_ref, k_ref, v_ref, kc_ref, vc_ref, o_ref = refs
    else:
        sink_ref, q_ref, kc_ref, vc_ref, o_ref = refs
    kh = pl.program_id(1)
    n = pl.program_id(2)
    rows = GQA_GROUP * BLOCK
    q = q_ref[...]
    qs = jnp.concatenate([q[:, g * HEAD_DIM:(g + 1) * HEAD_DIM] for g in range(GQA_GROUP)], axis=0)
    nt = (((1,), (1,)), ((), ()))
    s_ctx = lax.dot_general(qs, kc_ref[...], nt, preferred_element_type=F32)
    row = lax.broadcasted_iota(jnp.int32, (rows, 1), 0)
    sink = jnp.zeros((rows, 1), F32)
    for g in range(GQA_GROUP):
        sink = jnp.where(row // BLOCK == g, sink_ref[kh * GQA_GROUP + g], sink)
    m = jnp.maximum(jnp.max(s_ctx, axis=-1, keepdims=True), sink)
    if band:
        nband = 3 * BLOCK
        start = pl.multiple_of(jnp.clip((n - 1) * BLOCK, 0, seq - nband), BLOCK)
        kb = k_ref[pl.ds(start, nband), :]
        vb = v_ref[pl.ds(start, nband), :]
        s_band = lax.dot_general(qs, kb, nt, preferred_element_type=F32)
        qi = n * BLOCK + row % BLOCK
        kj = start + lax.broadcasted_iota(jnp.int32, (rows, nband), 1)
        s_band = jnp.where(jnp.abs(qi - kj) <= WINDOW, s_band, -jnp.inf)
        m = jnp.maximum(m, jnp.max(s_band, axis=-1, keepdims=True))
    p_ctx = jnp.exp(s_ctx - m)
    den = jnp.sum(p_ctx, axis=-1, keepdims=True) + jnp.exp(sink - m)
    o = jnp.dot(p_ctx.astype(BF16), vc_ref[...], preferred_element_type=F32)
    if band:
        p_band = jnp.exp(s_band - m)
        den = den + jnp.sum(p_band, axis=-1, keepdims=True)
        o = o + jnp.dot(p_band.astype(BF16), vb, preferred_element_type=F32)
    o = o / den
    for g in range(GQA_GROUP):
        o_ref[:, g * HEAD_DIM:(g + 1) * HEAD_DIM] = o[g * BLOCK:(g + 1) * BLOCK, :].astype(BF16)


def _attention(sink, q, kv, kvc, *, batch, seq, ctx_len, band):
    nqb = seq // BLOCK
    group_w = GQA_GROUP * HEAD_DIM
    in_specs = [
        pl.BlockSpec(memory_space=pltpu.SMEM),
        pl.BlockSpec((BLOCK, group_w), lambda b, k, n: (b * nqb + n, k)),
    ]
    args = [sink, q]
    if band:
        in_specs += [
            pl.BlockSpec((seq, HEAD_DIM), lambda b, k, n: (b, k)),
            pl.BlockSpec((seq, HEAD_DIM), lambda b, k, n: (b, N_KV_HEADS + k)),
        ]
        args += [kv, kv]
    in_specs += [
        pl.BlockSpec((ctx_len, HEAD_DIM), lambda b, k, n: (b, k)),
        pl.BlockSpec((ctx_len, HEAD_DIM), lambda b, k, n: (b, N_KV_HEADS + k)),
    ]
    args += [kvc, kvc]
    return pl.pallas_call(
        functools.partial(_attn_kernel, seq=seq, band=band),
        grid=(batch, N_KV_HEADS, nqb),
        in_specs=in_specs,
        out_specs=pl.BlockSpec((BLOCK, group_w), lambda b, k, n: (b * nqb + n, k)),
        out_shape=jax.ShapeDtypeStruct((batch * seq, ATTN_WIDTH), BF16),
        compiler_params=_cparams(3),
        name="attention",
    )(*args)


def _out_kernel(attn_ref, pv_ref, pvp_ref, pvn_ref, u_ref, v_ref, x_ref, mod_ref, pw_ref, ps_ref,
                sw_ref, sbt_ref, wo_ref, o_ref, pbuf_ref, mix_ref, *, tm, seq):
    m = pl.program_id(0)
    pos0 = (m * tm) % seq
    pbuf_ref[0:SUBLANES, :] = jnp.where(pos0 == 0, 0.0, pvp_ref[...])
    pbuf_ref[SUBLANES:SUBLANES + tm, :] = pv_ref[...]
    pbuf_ref[SUBLANES + tm:, :] = jnp.where(pos0 + tm == seq, 0.0, pvn_ref[...])
    pos = pos0 + lax.broadcasted_iota(jnp.int32, (tm, 1), 0)
    for gi, w in enumerate(POOL_WINDOWS):
        c0 = gi * POOL_GROUP_DIM
        cols = slice(c0, c0 + POOL_GROUP_DIM)
        acc = pbuf_ref[pl.ds(SUBLANES - w // 2, tm), cols]
        for dlt in range(1 - w // 2, w // 2):
            acc = acc + pbuf_ref[pl.ds(SUBLANES + dlt, tm), cols]
        lo = jnp.maximum(pos - w // 2, 0)
        hi = jnp.minimum(pos - w // 2 + w, seq)
        y = (acc / (hi - lo).astype(F32) - pv_ref[:, cols]).astype(BF16)
        z = jnp.dot(y, pw_ref[gi], preferred_element_type=F32) * ps_ref[:, cols]
        mix_ref[:, cols] = z.astype(BF16)
    n_chunks = tm // SGU_CHUNK
    for h in range(N_SGU_HEADS):
        cols = slice(h * LANES, (h + 1) * LANES)
        vh = jnp.concatenate(
            [v_ref[ci * SGU_CHUNK:(ci + 1) * SGU_CHUNK, cols] for ci in range(n_chunks)], axis=1)
        r = jnp.dot(sw_ref[h], vh, preferred_element_type=F32) + sbt_ref[:, h:h + 1]
        for ci in range(n_chunks):
            rs = slice(ci * SGU_CHUNK, (ci + 1) * SGU_CHUNK)
            gated = u_ref[rs, cols] * r[:, ci * SGU_CHUNK:(ci + 1) * SGU_CHUNK]
            mix_ref[rs, POOL_WIDTH + h * LANES:POOL_WIDTH + (h + 1) * LANES] = gated.astype(BF16)
    res = jnp.dot(attn_ref[...], wo_ref[0:ATTN_WIDTH, :], preferred_element_type=F32)
    res = res + jnp.dot(mix_ref[...], wo_ref[ATTN_WIDTH:, :], preferred_element_type=F32)
    o_ref[...] = x_ref[...] + mod_ref[2:3, :] * res


def _out_proj(attn, pv, u, v, x2, mod4, layer, mod_row, pool_w, pool_scale, sgu_w, sgu_bt, w_out,
              *, seq, tm):
    rows, d = x2.shape
    assert rows % tm == 0 and seq % tm == 0 and tm % SGU_CHUNK == 0
    per_seq = seq // tm
    hb = tm // SUBLANES
    last_hb = rows // SUBLANES - 1

    def mrow(m):
        return mod_row if mod_row is not None else m // per_seq

    full = lambda shape: pl.BlockSpec(shape, lambda m: (0,) * len(shape))
    return pl.pallas_call(
        functools.partial(_out_kernel, tm=tm, seq=seq),
        grid=(rows // tm,),
        in_specs=[
            pl.BlockSpec((tm, ATTN_WIDTH), lambda m: (m, 0)),
            pl.BlockSpec((tm, POOL_WIDTH), lambda m: (m, 0)),
            pl.BlockSpec((SUBLANES, POOL_WIDTH), lambda m: (jnp.maximum(m * hb - 1, 0), 0)),
            pl.BlockSpec((SUBLANES, POOL_WIDTH), lambda m: (jnp.minimum((m + 1) * hb, last_hb), 0)),
            pl.BlockSpec((tm, SGU_WIDTH), lambda m: (m, 0)),
            pl.BlockSpec((tm, SGU_WIDTH), lambda m: (m, 0)),
            pl.BlockSpec((tm, d), lambda m: (m, 0)),
            pl.BlockSpec((None, None, N_MOD, d), lambda m: (layer, mrow(m), 0, 0)),
            full(pool_w.shape),
            full(pool_scale.shape),
            full(sgu_w.shape),
            full(sgu_bt.shape),
            full(w_out.shape),
        ],
        out_specs=pl.BlockSpec((tm, d), lambda m: (m, 0)),
        out_shape=jax.ShapeDtypeStruct((rows, d), F32),
        scratch_shapes=[pltpu.VMEM((tm + 2 * SUBLANES, POOL_WIDTH), F32),
                        pltpu.VMEM((tm, POOL_WIDTH + SGU_WIDTH), BF16)],
        compiler_params=_cparams(1),
        name="out_proj",
    )(attn, pv, pv, pv, u, v, x2, mod4, pool_w, pool_scale, sgu_w, sgu_bt, w_out)


def _ffn_kernel(x_ref, xp_ref, xn_ref, mod_ref, g2_ref, wg_ref, wv_ref, cwg_ref, cwv_ref, cbg_ref,
                cbv_ref, wd_ref, fg_ref, o_ref, hs_ref, *, tm, seq, final):
    m = pl.program_id(0)
    f = pl.program_id(1)
    ext = tm + HALO
    chunk = 256

    @pl.when(f == 0)
    def _():
        shift = mod_ref[3:4, :]
        scale1 = 1.0 + mod_ref[4:5, :]
        g2 = g2_ref[...]

        def norm(xx):
            return _rms(xx, g2) * scale1 + shift

        def body(r):
            hs_ref[pl.ds(r, chunk), :] = norm(x_ref[pl.ds(r, chunk), :]).astype(BF16)
            o_ref[pl.ds(r, chunk), :] = jnp.zeros((chunk, o_ref.shape[1]), F32)

        _row_chunks(tm, chunk, body)
        pos0 = (m * tm) % seq
        hn = jnp.where(pos0 + tm == seq, 0.0, norm(xn_ref[...]))
        hp = jnp.where(pos0 == 0, 0.0, norm(xp_ref[...]))
        hs_ref[tm:, :] = jnp.concatenate([hn, hp], axis=0).astype(BF16)

    hs = hs_ref[...]

    def conv(w_ref, cw_ref, cb_ref):
        a = jnp.dot(hs, w_ref[...], preferred_element_type=F32)
        prev = pltpu.roll(a, 1, 0)[:tm]
        nxt = pltpu.roll(a, ext - 1, 0)[:tm]
        return cb_ref[...] + prev * cw_ref[0:1, :] + a[:tm] * cw_ref[1:2, :] + nxt * cw_ref[2:3, :]

    gate = conv(wg_ref, cwg_ref, cbg_ref)
    val = conv(wv_ref, cwv_ref, cbv_ref)
    act = (_silu(gate) * val).astype(BF16)
    o_ref[...] += jnp.dot(act, wd_ref[...], preferred_element_type=F32)

    @pl.when(f == pl.num_programs(1) - 1)
    def _():
        gate2 = mod_ref[5:6, :]
        fg = fg_ref[...]

        def body(r):
            y = x_ref[pl.ds(r, chunk), :] + gate2 * o_ref[pl.ds(r, chunk), :]
            if final:
                y = _rms(y, fg)
            o_ref[pl.ds(r, chunk), :] = y

        _row_chunks(tm, chunk, body)


def _ffn(x2, mod4, layer, mod_row, g2, w_up, conv_w, conv_b, w_down, final_g, *, seq, tm, final):
    rows, d = x2.shape
    assert rows % tm == 0 and seq % tm == 0 and tm % BF16_SUBLANES == 0
    per_seq = seq // tm
    hb = tm // SUBLANES
    last_hb = rows // SUBLANES - 1
    nf = D_FF // FF_TILE

    def mrow(m):
        return mod_row if mod_row is not None else m // per_seq

    return pl.pallas_call(
        functools.partial(_ffn_kernel, tm=tm, seq=seq, final=final),
        grid=(rows // tm, nf),
        in_specs=[
            pl.BlockSpec((tm, d), lambda m, f: (m, 0)),
            pl.BlockSpec((SUBLANES, d), lambda m, f: (jnp.maximum(m * hb - 1, 0), 0)),
            pl.BlockSpec((SUBLANES, d), lambda m, f: (jnp.minimum((m + 1) * hb, last_hb), 0)),
            pl.BlockSpec((None, None, N_MOD, d), lambda m, f: (layer, mrow(m), 0, 0)),
            pl.BlockSpec((1, d), lambda m, f: (0, 0)),
            pl.BlockSpec((d, FF_TILE), lambda m, f: (0, f)),
            pl.BlockSpec((d, FF_TILE), lambda m, f: (0, f + nf)),
            pl.BlockSpec((3, FF_TILE), lambda m, f: (0, f)),
            pl.BlockSpec((3, FF_TILE), lambda m, f: (0, f + nf)),
            pl.BlockSpec((1, FF_TILE), lambda m, f: (0, f)),
            pl.BlockSpec((1, FF_TILE), lambda m, f: (0, f + nf)),
            pl.BlockSpec((FF_TILE, d), lambda m, f: (f, 0)),
            pl.BlockSpec((1, d), lambda m, f: (0, 0)),
        ],
        out_specs=pl.BlockSpec((tm, d), lambda m, f: (m, 0)),
        out_shape=jax.ShapeDtypeStruct((rows, d), F32),
        scratch_shapes=[pltpu.VMEM((tm + HALO, d), BF16)],
        compiler_params=_cparams(2),
        name="ffn",
    )(x2, x2, x2, mod4, g2, w_up, w_up, conv_w, conv_w, conv_b, conv_b, w_down, final_g)


def _rope_tables(seq):
    rows = seq // GRID_W
    row_ids = jnp.repeat(jnp.arange(rows), GRID_W).astype(F32)
    col_ids = jnp.tile(jnp.arange(GRID_W), rows).astype(F32)
    inv = 1.0 / (ROPE_BASE ** (jnp.arange(0, ROT_AXIS_DIM, 2, dtype=F32) / ROT_AXIS_DIM))
    ang_r = row_ids[:, None] * inv
    ang_c = col_ids[:, None] * inv
    cr, sr, cc, sc = jnp.cos(ang_r), jnp.sin(ang_r), jnp.cos(ang_c), jnp.sin(ang_c)
    cos = jnp.concatenate([cr, cr, cc, cc], axis=-1)
    sin_signed = jnp.concatenate([-sr, sr, -sc, sc], axis=-1)
    return cos, sin_signed


def kernel(x, c, ctx, c_ctx, norm1_g, norm2_g, w_ada, b_ada, w_in, q_norm_g, k_norm_g, attn_sink,
           pool_w, pool_scale, sgu_norm_g, sgu_w, sgu_b, w_out, w_up, conv_w, conv_b, w_down,
           final_norm_g):
    batch, seq, d = x.shape
    ctx_len = ctx.shape[1]
    depth = w_ada.shape[0]
    assert d == D_MODEL and batch + 1 <= MOD_ROWS
    ctx_row = batch

    cc = jnp.zeros((MOD_ROWS, d), F32).at[:batch].set(c).at[ctx_row].set(c_ctx)
    mod4 = _modulation(cc, w_ada, b_ada).reshape(depth, MOD_ROWS, N_MOD, d)
    rope_tabs = _rope_tables(seq)

    w_in_b, w_out_b, w_up_b, w_down_b = (w.astype(BF16) for w in (w_in, w_out, w_up, w_down))
    pool_w_b, sgu_w_b = pool_w.astype(BF16), sgu_w.astype(BF16)

    x2 = x.reshape(batch * seq, d)
    xc2 = ctx.reshape(batch * ctx_len, d)
    lat_tm = 512
    ctx_tm = ctx_len
    for l in range(depth):
        last = l == depth - 1
        g1, g2 = norm1_g[l][None], norm2_g[l][None]
        qg, kg, sg = q_norm_g[l][None], k_norm_g[l][None], sgu_norm_g[l][None]
        mixer = (pool_w_b[l], pool_scale[l][None], sgu_w_b[l], sgu_b[l].T, w_out_b[l])
        mlp = (g2, w_up_b[l], conv_w[l], conv_b[l][None], w_down_b[l], final_norm_g[None])

        q, kv, pv, u, v = _in_proj(x2, mod4, l, None, g1, w_in_b[l], qg, kg, sg, rope_tabs,
                                   seq=seq, tm=lat_tm, tiles=range(6))
        if last:
            (kvc,) = _in_proj(xc2, mod4, l, ctx_row, g1, w_in_b[l], qg, kg, sg, None,
                              seq=ctx_len, tm=ctx_tm, tiles=(2,))
        else:
            qc, kvc, pvc, uc, vc = _in_proj(xc2, mod4, l, ctx_row, g1, w_in_b[l], qg, kg, sg, None,
                                            seq=ctx_len, tm=ctx_tm, tiles=range(6))
        attn = _attention(attn_sink[l], q, kv, kvc, batch=batch, seq=seq, ctx_len=ctx_len, band=True)
        x2 = _out_proj(attn, pv, u, v, x2, mod4, l, None, *mixer, seq=seq, tm=lat_tm)
        if not last:
            attn_c = _attention(attn_sink[l], qc, None, kvc, batch=batch, seq=ctx_len,
                                ctx_len=ctx_len, band=False)
            xc2 = _out_proj(attn_c, pvc, uc, vc, xc2, mod4, l, ctx_row, *mixer, seq=ctx_len, tm=ctx_tm)
        x2 = _ffn(x2, mod4, l, None, *mlp, seq=seq, tm=lat_tm, final=last)
        if not last:
            xc2 = _ffn(xc2, mod4, l, ctx_row, *mlp, seq=ctx_len, tm=ctx_tm, final=False)
    return x2.reshape(batch, seq, d)
```

```python
import functools

import jax
import jax.numpy as jnp
from jax import lax
from jax.experimental import pallas as pl
from jax.experimental.pallas import tpu as pltpu

F32 = jnp.float32
BF16 = jnp.bfloat16

D_MODEL = 2048
GRID_W = 64
HEAD_DIM = 128
N_Q_HEADS = 8
N_KV_HEADS = 2
GQA_GROUP = N_Q_HEADS // N_KV_HEADS
ATTN_WIDTH = N_Q_HEADS * HEAD_DIM
KV_WIDTH = N_KV_HEADS * HEAD_DIM
WINDOW = 128
BLOCK = 128
ROPE_BASE = 10000.0
ROT_AXIS_DIM = HEAD_DIM // 2
ROT_PAIR = ROT_AXIS_DIM // 2
POOL_WINDOWS = (2, 4, 8, 16)
POOL_WIDTH = 512
POOL_GROUP_DIM = 128
SGU_WIDTH = 512
N_SGU_HEADS = 4
SGU_CHUNK = 128
IN_WIDTH = 3072
D_FF = 5632
N_MOD = 6
EPS = 1e-6

SUBLANES = 8
BF16_SUBLANES = 16
LANES = 128
HALO = BF16_SUBLANES
MOD_ROWS = 16
IN_TILE = 512
FF_TILE = 512
FF_PARTS = 2
VMEM_LIMIT = 56 * 1024 * 1024


def _cparams(n_axes):
    return pltpu.CompilerParams(
        dimension_semantics=("arbitrary",) * n_axes, vmem_limit_bytes=VMEM_LIMIT)


def _rms(x, g):
    return x * lax.rsqrt(jnp.mean(x * x, axis=-1, keepdims=True) + EPS) * g


def _gelu(x):
    return 0.5 * x * (1.0 + lax.erf(x * 0.7071067811865476))


def _silu(x):
    return x * jax.nn.sigmoid(x)


def _norm_rows(dst_ref, x_ref, n_rows, gain, shift, *, dst_offset=0):
    rows = BF16_SUBLANES
    for r in range(0, n_rows, rows):
        xx = x_ref[r:r + rows, :]
        inv = lax.rsqrt(jnp.mean(xx * xx, axis=-1, keepdims=True) + EPS)
        dst_ref[dst_offset + r:dst_offset + r + rows, :] = (xx * inv * gain + shift).astype(BF16)


def _resident(shape, index_map):
    return pl.BlockSpec(shape, index_map, pipeline_mode=pl.Buffered(1))


def _mod_kernel(c_ref, w_ref, b_ref, o_ref):
    c = c_ref[...]
    s = _silu(c).astype(BF16)
    o_ref[...] = jnp.dot(s, w_ref[...].astype(BF16), preferred_element_type=F32) + b_ref[...]


def _modulation(cc, w_ada, b_ada):
    depth, d, n = w_ada.shape
    tn = 1024
    return pl.pallas_call(
        _mod_kernel,
        grid=(depth, n // tn),
        in_specs=[
            pl.BlockSpec((MOD_ROWS, d), lambda l, j: (0, 0)),
            pl.BlockSpec((None, d, tn), lambda l, j: (l, 0, j)),
            pl.BlockSpec((None, 1, tn), lambda l, j: (l, 0, j)),
        ],
        out_specs=pl.BlockSpec((None, MOD_ROWS, tn), lambda l, j: (l, 0, j)),
        out_shape=jax.ShapeDtypeStruct((depth, MOD_ROWS, n), F32),
        compiler_params=_cparams(2),
        name="modulation",
    )(cc, w_ada, b_ada.reshape(depth, 1, n))


def _in_kernel(*refs, tm, tiles, rope):
    n_in = 9 if rope else 7
    x_ref, mod_ref, g1_ref, w_ref, qg_ref, kg_ref, sg_ref = refs[:7]
    cos_ref, sin_ref = refs[7:9] if rope else (None, None)
    out_refs = refs[n_in:-1]
    hs_ref = refs[-1]
    names = [nm for nm, ids in (("q", (0, 1)), ("kv", (2,)), ("pv", (3,)), ("u", (4,)), ("v", (5,)))
             if any(t in tiles for t in ids)]
    outs = dict(zip(names, out_refs))
    grp = 128

    gain = g1_ref[...] * (1.0 + mod_ref[1:2, :])
    _norm_rows(hs_ref, x_ref, tm, gain, mod_ref[0:1, :])
    hs = hs_ref[...]

    def project(i):
        return jnp.dot(hs, w_ref[:, i * IN_TILE:(i + 1) * IN_TILE], preferred_element_type=F32)

    def head(acc, r, h, g):
        xh = _rms(acc[r:r + grp, h * HEAD_DIM:(h + 1) * HEAD_DIM], g)
        if rope:
            xh = xh * cos_ref[r:r + grp, :] + pltpu.roll(xh, ROT_AXIS_DIM, 1) * sin_ref[r:r + grp, :]
        return xh.astype(BF16)

    def epilogue(t, acc):
        if t in (0, 1):
            qg = qg_ref[...] * (HEAD_DIM ** -0.5)
            for r in range(0, tm, grp):
                for h in range(IN_TILE // HEAD_DIM):
                    c0 = t * IN_TILE + h * HEAD_DIM
                    outs["q"][r:r + grp, c0:c0 + HEAD_DIM] = head(acc, r, h, qg)
        elif t == 2:
            kg = kg_ref[...]
            for r in range(0, tm, grp):
                for h in range(N_KV_HEADS):
                    outs["kv"][r:r + grp, h * HEAD_DIM:(h + 1) * HEAD_DIM] = head(acc, r, h, kg)
                outs["kv"][r:r + grp, KV_WIDTH:] = acc[r:r + grp, KV_WIDTH:].astype(BF16)
        elif t == 3:
            outs["pv"][...] = acc
        elif t == 4:
            for r in range(0, tm, grp):
                outs["u"][r:r + grp, :] = _gelu(acc[r:r + grp, :])
        else:
            sg = sg_ref[...]
            for r in range(0, tm, grp):
                outs["v"][r:r + grp, :] = _rms(_gelu(acc[r:r + grp, :]), sg).astype(BF16)

    nxt = project(0)
    for i, t in enumerate(tiles):
        acc = nxt
        if i + 1 < len(tiles):
            nxt = project(i + 1)
        epilogue(t, acc)


def _in_proj(x2, mod4, layer, mod_row, g1, w_in, qg, kg, sg, rope_tabs, *, seq, tm, tiles):
    rows, d = x2.shape
    assert rows % tm == 0 and seq % tm == 0
    tiles = tuple(tiles)
    assert tiles == tuple(range(tiles[0], tiles[0] + len(tiles))) and tiles[0] % len(tiles) == 0
    width = len(tiles) * IN_TILE
    wblk = tiles[0] // len(tiles)
    rope = rope_tabs is not None
    per_seq = seq // tm

    def mrow(m):
        return mod_row if mod_row is not None else m // per_seq

    in_specs = [
        pl.BlockSpec((tm, d), lambda m: (m, 0)),
        pl.BlockSpec((None, None, N_MOD, d), lambda m: (layer, mrow(m), 0, 0)),
        pl.BlockSpec((None, 1, d), lambda m: (layer, 0, 0)),
        _resident((None, d, width), lambda m: (layer, 0, wblk)),
        pl.BlockSpec((None, 1, HEAD_DIM), lambda m: (layer, 0, 0)),
        pl.BlockSpec((None, 1, HEAD_DIM), lambda m: (layer, 0, 0)),
        pl.BlockSpec((None, 1, SGU_WIDTH), lambda m: (layer, 0, 0)),
    ]
    args = [x2, mod4, g1, w_in, qg, kg, sg]
    if rope:
        in_specs += [pl.BlockSpec((tm, HEAD_DIM), lambda m: (m % per_seq, 0)) for _ in range(2)]
        args += list(rope_tabs)
    out_specs, out_shape = [], []

    def add(width, dtype):
        out_specs.append(pl.BlockSpec((tm, width), lambda m: (m, 0)))
        out_shape.append(jax.ShapeDtypeStruct((rows, width), dtype))

    if 0 in tiles or 1 in tiles:
        assert 0 in tiles and 1 in tiles
        add(ATTN_WIDTH, BF16)
    if 2 in tiles:
        add(IN_TILE, BF16)
    if 3 in tiles:
        add(POOL_WIDTH, F32)
    if 4 in tiles:
        add(SGU_WIDTH, F32)
    if 5 in tiles:
        add(SGU_WIDTH, BF16)
    return pl.pallas_call(
        functools.partial(_in_kernel, tm=tm, tiles=tiles, rope=rope),
        grid=(rows // tm,),
        in_specs=in_specs,
        out_specs=out_specs,
        out_shape=out_shape,
        scratch_shapes=[pltpu.VMEM((tm, d), BF16)],
        compiler_params=_cparams(1),
        name="in_proj",
    )(*args)


def _attn_kernel(*refs, seq, band):
    if band:
        sink_ref, q_ref, kv_ref, kvc_ref, o_ref = refs
    else:
        sink_ref, q_ref, kvc_ref, o_ref = refs
    n = pl.program_id(1)
    rows = GQA_GROUP * BLOCK
    nt = (((1,), (1,)), ((), ()))
    row = lax.broadcasted_iota(jnp.int32, (rows, 1), 0)
    if band:
        nband = 3 * BLOCK
        start = pl.multiple_of(jnp.clip((n - 1) * BLOCK, 0, seq - nband), BLOCK)
        dist = (n * BLOCK - start) + (lax.broadcasted_iota(jnp.int32, (BLOCK, nband), 0)
                                      - lax.broadcasted_iota(jnp.int32, (BLOCK, nband), 1))
        bias = jnp.where(jnp.abs(dist) <= WINDOW, 0.0, -jnp.inf).astype(F32)
        bias = jnp.concatenate([bias] * GQA_GROUP, axis=0)
    for kh in range(N_KV_HEADS):
        q0 = kh * GQA_GROUP * HEAD_DIM
        kcol = slice(kh * HEAD_DIM, (kh + 1) * HEAD_DIM)
        vcol = slice(KV_WIDTH + kh * HEAD_DIM, KV_WIDTH + (kh + 1) * HEAD_DIM)
        qs = jnp.concatenate(
            [q_ref[:, q0 + g * HEAD_DIM:q0 + (g + 1) * HEAD_DIM] for g in range(GQA_GROUP)], axis=0)
        s_ctx = lax.dot_general(qs, kvc_ref[:, kcol], nt, preferred_element_type=F32)
        sink = jnp.zeros((rows, 1), F32)
        for g in range(GQA_GROUP):
            sink = jnp.where(row // BLOCK == g, sink_ref[kh * GQA_GROUP + g], sink)
        m = jnp.maximum(jnp.max(s_ctx, axis=-1, keepdims=True), sink)
        if band:
            s_band = lax.dot_general(qs, kv_ref[pl.ds(start, nband), kcol], nt,
                                     preferred_element_type=F32) + bias
            m = jnp.maximum(m, jnp.max(s_band, axis=-1, keepdims=True))
        p_ctx = jnp.exp(s_ctx - m)
        den = jnp.sum(p_ctx, axis=-1, keepdims=True) + jnp.exp(sink - m)
        o = jnp.dot(p_ctx.astype(BF16), kvc_ref[:, vcol], preferred_element_type=F32)
        if band:
            p_band = jnp.exp(s_band - m)
            den = den + jnp.sum(p_band, axis=-1, keepdims=True)
            o = o + jnp.dot(p_band.astype(BF16), kv_ref[pl.ds(start, nband), vcol],
                            preferred_element_type=F32)
        o = o / den
        for g in range(GQA_GROUP):
            o_ref[:, q0 + g * HEAD_DIM:q0 + (g + 1) * HEAD_DIM] = (
                o[g * BLOCK:(g + 1) * BLOCK, :].astype(BF16))


def _attention(sink, q, kv, kvc, *, batch, seq, ctx_len, band):
    nqb = seq // BLOCK
    in_specs = [
        pl.BlockSpec(memory_space=pltpu.SMEM),
        pl.BlockSpec((BLOCK, ATTN_WIDTH), lambda b, n: (b * nqb + n, 0)),
    ]
    args = [sink, q]
    if band:
        in_specs.append(pl.BlockSpec((seq, 2 * KV_WIDTH), lambda b, n: (b, 0)))
        args.append(kv)
    in_specs.append(pl.BlockSpec((ctx_len, 2 * KV_WIDTH), lambda b, n: (b, 0)))
    args.append(kvc)
    return pl.pallas_call(
        functools.partial(_attn_kernel, seq=seq, band=band),
        grid=(batch, nqb),
        in_specs=in_specs,
        out_specs=pl.BlockSpec((BLOCK, ATTN_WIDTH), lambda b, n: (b * nqb + n, 0)),
        out_shape=jax.ShapeDtypeStruct((batch * seq, ATTN_WIDTH), BF16),
        compiler_params=_cparams(2),
        name="attention",
    )(*args)


def _out_kernel(attn_ref, pv_ref, pvp_ref, pvn_ref, u_ref, v_ref, x_ref, mod_ref, pw_ref, ps_ref,
                sw_ref, sbt_ref, wo_ref, o_ref, pbuf_ref, mix_ref, *, tm, seq):
    m = pl.program_id(0)
    pos0 = (m * tm) % seq
    pbuf_ref[0:SUBLANES, :] = jnp.where(pos0 == 0, 0.0, pvp_ref[...])
    pbuf_ref[SUBLANES:SUBLANES + tm, :] = pv_ref[...]
    pbuf_ref[SUBLANES + tm:, :] = jnp.where(pos0 + tm == seq, 0.0, pvn_ref[...])
    pos = pos0 + lax.broadcasted_iota(jnp.int32, (tm, 1), 0)
    for gi, w in enumerate(POOL_WINDOWS):
        c0 = gi * POOL_GROUP_DIM
        cols = slice(c0, c0 + POOL_GROUP_DIM)
        acc = pbuf_ref[pl.ds(SUBLANES - w // 2, tm), cols]
        for dlt in range(1 - w // 2, w // 2):
            acc = acc + pbuf_ref[pl.ds(SUBLANES + dlt, tm), cols]
        lo = jnp.maximum(pos - w // 2, 0)
        hi = jnp.minimum(pos - w // 2 + w, seq)
        y = (acc / (hi - lo).astype(F32) - pv_ref[:, cols]).astype(BF16)
        z = jnp.dot(y, pw_ref[gi], preferred_element_type=F32) * ps_ref[:, cols]
        mix_ref[:, cols] = z.astype(BF16)
    n_chunks = tm // SGU_CHUNK
    for h in range(N_SGU_HEADS):
        cols = slice(h * LANES, (h + 1) * LANES)
        vh = jnp.concatenate(
            [v_ref[ci * SGU_CHUNK:(ci + 1) * SGU_CHUNK, cols] for ci in range(n_chunks)], axis=1)
        r = jnp.dot(sw_ref[h], vh, preferred_element_type=F32) + sbt_ref[:, h:h + 1]
        for ci in range(n_chunks):
            rs = slice(ci * SGU_CHUNK, (ci + 1) * SGU_CHUNK)
            gated = u_ref[rs, cols] * r[:, ci * SGU_CHUNK:(ci + 1) * SGU_CHUNK]
            mix_ref[rs, POOL_WIDTH + h * LANES:POOL_WIDTH + (h + 1) * LANES] = gated.astype(BF16)
    res = jnp.dot(attn_ref[...], wo_ref[0:ATTN_WIDTH, :], preferred_element_type=F32)
    res = res + jnp.dot(mix_ref[...], wo_ref[ATTN_WIDTH:, :], preferred_element_type=F32)
    o_ref[...] = x_ref[...] + mod_ref[2:3, :] * res


def _out_proj(attn, pv, u, v, x2, mod4, layer, mod_row, pool_w, pool_scale, sgu_w, sgu_bt, w_out,
              *, seq, tm):
    rows, d = x2.shape
    assert rows % tm == 0 and seq % tm == 0 and tm % SGU_CHUNK == 0
    per_seq = seq // tm
    hb = tm // SUBLANES
    last_hb = rows // SUBLANES - 1

    def mrow(m):
        return mod_row if mod_row is not None else m // per_seq

    def layer_block(a):
        shape = a.shape[1:]
        return pl.BlockSpec((None,) + shape, lambda m: (layer,) + (0,) * len(shape))

    return pl.pallas_call(
        functools.partial(_out_kernel, tm=tm, seq=seq),
        grid=(rows // tm,),
        in_specs=[
            pl.BlockSpec((tm, ATTN_WIDTH), lambda m: (m, 0)),
            pl.BlockSpec((tm, POOL_WIDTH), lambda m: (m, 0)),
            pl.BlockSpec((SUBLANES, POOL_WIDTH), lambda m: (jnp.maximum(m * hb - 1, 0), 0)),
            pl.BlockSpec((SUBLANES, POOL_WIDTH), lambda m: (jnp.minimum((m + 1) * hb, last_hb), 0)),
            pl.BlockSpec((tm, SGU_WIDTH), lambda m: (m, 0)),
            pl.BlockSpec((tm, SGU_WIDTH), lambda m: (m, 0)),
            pl.BlockSpec((tm, d), lambda m: (m, 0)),
            pl.BlockSpec((None, None, N_MOD, d), lambda m: (layer, mrow(m), 0, 0)),
            layer_block(pool_w),
            layer_block(pool_scale),
            layer_block(sgu_w),
            layer_block(sgu_bt),
            _resident((None,) + w_out.shape[1:], lambda m: (layer, 0, 0)),
        ],
        out_specs=pl.BlockSpec((tm, d), lambda m: (m, 0)),
        out_shape=jax.ShapeDtypeStruct((rows, d), F32),
        scratch_shapes=[pltpu.VMEM((tm + 2 * SUBLANES, POOL_WIDTH), F32),
                        pltpu.VMEM((tm, POOL_WIDTH + SGU_WIDTH), BF16)],
        compiler_params=_cparams(1),
        name="out_proj",
    )(attn, pv, pv, pv, u, v, x2, mod4, pool_w, pool_scale, sgu_w, sgu_bt, w_out)


def _ffn_kernel(x_ref, xp_ref, xn_ref, mod_ref, g2_ref, wg_ref, wv_ref, cwg_ref, cwv_ref, cbg_ref,
                cbv_ref, wd_ref, fg_ref, o_ref, hs_ref, ag_ref, av_ref, act_ref, *, tm, seq, final):
    m = pl.program_id(0)
    f = pl.program_id(1)
    rows = BF16_SUBLANES
    grp = 128

    @pl.when(f == 0)
    def _():
        shift = mod_ref[3:4, :]
        gain = g2_ref[...] * (1.0 + mod_ref[4:5, :])
        _norm_rows(hs_ref, x_ref, tm, gain, shift, dst_offset=HALO)

        def halo(ref):
            xx = ref[...]
            inv = lax.rsqrt(jnp.mean(xx * xx, axis=-1, keepdims=True) + EPS)
            return xx * inv * gain + shift

        pos0 = (m * tm) % seq
        hn = jnp.where(pos0 + tm == seq, 0.0, halo(xn_ref))
        hp = jnp.where(pos0 == 0, 0.0, halo(xp_ref))
        pad = jnp.zeros_like(hp)
        hs_ref[0:HALO, :] = jnp.concatenate([pad, hp], axis=0).astype(BF16)
        hs_ref[HALO + tm:, :] = jnp.concatenate([hn, pad], axis=0).astype(BF16)
        o_ref[...] = jnp.zeros(o_ref.shape, F32)

    hs = hs_ref[...]
    part = FF_TILE // FF_PARTS
    parts = [slice(p * part, (p + 1) * part) for p in range(FF_PARTS)]
    for cs in parts:
        ag_ref[:, cs] = jnp.dot(hs, wg_ref[:, cs], preferred_element_type=F32)
        av_ref[:, cs] = jnp.dot(hs, wv_ref[:, cs], preferred_element_type=F32)

    def conv(a_ref, cw_ref, cb_ref, r, cs):
        prev = a_ref[HALO - 1 + r:HALO - 1 + r + grp, cs]
        cur = a_ref[HALO + r:HALO + r + grp, cs]
        nxt = a_ref[HALO + 1 + r:HALO + 1 + r + grp, cs]
        return cb_ref[:, cs] + prev * cw_ref[0:1, cs] + cur * cw_ref[1:2, cs] + nxt * cw_ref[2:3, cs]

    down = None
    for cs in parts:
        for r in range(0, tm, grp):
            act = _silu(conv(ag_ref, cwg_ref, cbg_ref, r, cs)) * conv(av_ref, cwv_ref, cbv_ref, r, cs)
            act_ref[r:r + grp, cs] = act.astype(BF16)
        contrib = jnp.dot(act_ref[:, cs], wd_ref[cs, :], preferred_element_type=F32)
        down = contrib if down is None else down + contrib
    o_ref[...] += down

    @pl.when(f == pl.num_programs(1) - 1)
    def _():
        gate2 = mod_ref[5:6, :]
        fg = fg_ref[...]

        for r in range(0, tm, rows):
            y = x_ref[r:r + rows, :] + gate2 * o_ref[r:r + rows, :]
            if final:
                y = _rms(y, fg)
            o_ref[r:r + rows, :] = y


def _ffn(x2, mod4, layer, mod_row, g2, w_up, conv_w, conv_b, w_down, final_g, *, seq, tm, final):
    rows, d = x2.shape
    assert rows % tm == 0 and seq % tm == 0 and tm % BF16_SUBLANES == 0
    per_seq = seq // tm
    hb = tm // SUBLANES
    last_hb = rows // SUBLANES - 1
    nf = D_FF // FF_TILE

    def mrow(m):
        return mod_row if mod_row is not None else m // per_seq

    return pl.pallas_call(
        functools.partial(_ffn_kernel, tm=tm, seq=seq, final=final),
        grid=(rows // tm, nf),
        in_specs=[
            pl.BlockSpec((tm, d), lambda m, f: (m, 0)),
            pl.BlockSpec((SUBLANES, d), lambda m, f: (jnp.maximum(m * hb - 1, 0), 0)),
            pl.BlockSpec((SUBLANES, d), lambda m, f: (jnp.minimum((m + 1) * hb, last_hb), 0)),
            pl.BlockSpec((None, None, N_MOD, d), lambda m, f: (layer, mrow(m), 0, 0)),
            pl.BlockSpec((None, 1, d), lambda m, f: (layer, 0, 0)),
            pl.BlockSpec((None, d, FF_TILE), lambda m, f: (layer, 0, f)),
            pl.BlockSpec((None, d, FF_TILE), lambda m, f: (layer, 0, f + nf)),
            pl.BlockSpec((None, 3, FF_TILE), lambda m, f: (layer, 0, f)),
            pl.BlockSpec((None, 3, FF_TILE), lambda m, f: (layer, 0, f + nf)),
            pl.BlockSpec((None, 1, FF_TILE), lambda m, f: (layer, 0, f)),
            pl.BlockSpec((None, 1, FF_TILE), lambda m, f: (layer, 0, f + nf)),
            pl.BlockSpec((None, FF_TILE, d), lambda m, f: (layer, f, 0)),
            pl.BlockSpec((1, d), lambda m, f: (0, 0)),
        ],
        out_specs=pl.BlockSpec((tm, d), lambda m, f: (m, 0)),
        out_shape=jax.ShapeDtypeStruct((rows, d), F32),
        scratch_shapes=[pltpu.VMEM((tm + 2 * HALO, d), BF16),
                        pltpu.VMEM((tm + 2 * HALO, FF_TILE), F32),
                        pltpu.VMEM((tm + 2 * HALO, FF_TILE), F32),
                        pltpu.VMEM((tm, FF_TILE), BF16)],
        compiler_params=_cparams(2),
        name="ffn",
    )(x2, x2, x2, mod4, g2, w_up, w_up, conv_w, conv_w, conv_b, conv_b, w_down, final_g)


def _head_lane_order(a, n_heads):
    lead = a.shape[:-1]
    a = a.reshape(lead + (n_heads, 2, 2, ROT_PAIR))
    return jnp.swapaxes(a, -2, -3).reshape(lead + (n_heads * HEAD_DIM,))


def _rope_tables(seq):
    rows = seq // GRID_W
    row_ids = jnp.repeat(jnp.arange(rows), GRID_W).astype(F32)
    col_ids = jnp.tile(jnp.arange(GRID_W), rows).astype(F32)
    inv = 1.0 / (ROPE_BASE ** (jnp.arange(0, ROT_AXIS_DIM, 2, dtype=F32) / ROT_AXIS_DIM))
    ang_r = row_ids[:, None] * inv
    ang_c = col_ids[:, None] * inv
    cr, sr, cc, sc = jnp.cos(ang_r), jnp.sin(ang_r), jnp.cos(ang_c), jnp.sin(ang_c)
    cos = jnp.concatenate([cr, cc, cr, cc], axis=-1)
    sin_signed = jnp.concatenate([-sr, -sc, sr, sc], axis=-1)
    return cos, sin_signed


def kernel(x, c, ctx, c_ctx, norm1_g, norm2_g, w_ada, b_ada, w_in, q_norm_g, k_norm_g, attn_sink,
           pool_w, pool_scale, sgu_norm_g, sgu_w, sgu_b, w_out, w_up, conv_w, conv_b, w_down,
           final_norm_g):
    batch, seq, d = x.shape
    ctx_len = ctx.shape[1]
    depth = w_ada.shape[0]
    assert d == D_MODEL and batch + 1 <= MOD_ROWS
    ctx_row = batch

    cc = jnp.zeros((MOD_ROWS, d), F32).at[:batch].set(c).at[ctx_row].set(c_ctx)
    mod4 = _modulation(cc, w_ada, b_ada).reshape(depth, MOD_ROWS, N_MOD, d)
    rope_tabs = _rope_tables(seq)

    qk = ATTN_WIDTH + KV_WIDTH
    w_in_b = jnp.concatenate(
        [_head_lane_order(w_in[..., :qk], N_Q_HEADS + N_KV_HEADS), w_in[..., qk:]], axis=-1
    ).astype(BF16)
    w_out_b, w_up_b, w_down_b = (w.astype(BF16) for w in (w_out, w_up, w_down))
    pool_w_b, sgu_w_b = pool_w.astype(BF16), sgu_w.astype(BF16)
    g1, g2 = norm1_g[:, None], norm2_g[:, None]
    qg = _head_lane_order(q_norm_g, 1)[:, None]
    kg = _head_lane_order(k_norm_g, 1)[:, None]
    sg = sgu_norm_g[:, None]
    mixer = (pool_w_b, pool_scale[:, None], sgu_w_b, jnp.swapaxes(sgu_b, 1, 2), w_out_b)
    mlp = (g2, w_up_b, conv_w, conv_b[:, None], w_down_b, final_norm_g[None])

    x2 = x.reshape(batch * seq, d)
    xc2 = ctx.reshape(batch * ctx_len, d)
    lat_tm = 512
    ctx_tm = ctx_len
    for l in range(depth):
        last = l == depth - 1
        q, kv, pv, u, v = _in_proj(x2, mod4, l, None, g1, w_in_b, qg, kg, sg, rope_tabs,
                                   seq=seq, tm=lat_tm, tiles=range(6))
        if last:
            (kvc,) = _in_proj(xc2, mod4, l, ctx_row, g1, w_in_b, qg, kg, sg, None,
                              seq=ctx_len, tm=ctx_tm, tiles=(2,))
        else:
            qc, kvc, pvc, uc, vc = _in_proj(xc2, mod4, l, ctx_row, g1, w_in_b, qg, kg, sg, None,
                                            seq=ctx_len, tm=ctx_tm, tiles=range(6))
        attn = _attention(attn_sink[l], q, kv, kvc, batch=batch, seq=seq, ctx_len=ctx_len, band=True)
        x2 = _out_proj(attn, pv, u, v, x2, mod4, l, None, *mixer, seq=seq, tm=lat_tm)
        if not last:
            attn_c = _attention(attn_sink[l], qc, None, kvc, batch=batch, seq=ctx_len,
                                ctx_len=ctx_len, band=False)
            xc2 = _out_proj(attn_c, pvc, uc, vc, xc2, mod4, l, ctx_row, *mixer, seq=ctx_len, tm=ctx_tm)
        x2 = _ffn(x2, mod4, l, None, *mlp, seq=seq, tm=lat_tm, final=last)
        if not last:
            xc2 = _ffn(xc2, mod4, l, ctx_row, *mlp, seq=ctx_len, tm=ctx_tm, final=False)
    return x2.reshape(batch, seq, d)
```

```python
import functools

import jax
import jax.numpy as jnp
from jax import lax
from jax.experimental import pallas as pl
from jax.experimental.pallas import tpu as pltpu

F32 = jnp.float32
BF16 = jnp.bfloat16

D_MODEL = 2048
GRID_W = 64
HEAD_DIM = 128
N_Q_HEADS = 8
N_KV_HEADS = 2
GQA_GROUP = N_Q_HEADS // N_KV_HEADS
ATTN_WIDTH = N_Q_HEADS * HEAD_DIM
KV_WIDTH = N_KV_HEADS * HEAD_DIM
WINDOW = 128
BLOCK = 128
ROPE_BASE = 10000.0
ROT_AXIS_DIM = HEAD_DIM // 2
ROT_PAIR = ROT_AXIS_DIM // 2
POOL_WINDOWS = (2, 4, 8, 16)
POOL_WIDTH = 512
POOL_GROUP_DIM = 128
SGU_WIDTH = 512
N_SGU_HEADS = 4
SGU_CHUNK = 128
IN_WIDTH = 3072
D_FF = 5632
N_MOD = 6
EPS = 1e-6

SUBLANES = 8
BF16_SUBLANES = 16
LANES = 128
HALO = BF16_SUBLANES
MOD_ROWS = 16
IN_TILE = 512
FF_TILE = 512
FF_PARTS = 2
FF_DOWN_ROWS = 256
VMEM_LIMIT = 60 * 1024 * 1024


def _cparams(n_axes):
    return pltpu.CompilerParams(
        dimension_semantics=("arbitrary",) * n_axes, vmem_limit_bytes=VMEM_LIMIT)


def _rms(x, g):
    return x * lax.rsqrt(jnp.mean(x * x, axis=-1, keepdims=True) + EPS) * g


def _gelu(x):
    return 0.5 * x * (1.0 + lax.erf(x * 0.7071067811865476))


def _silu(x):
    return x * jax.nn.sigmoid(x)


def _norm_rows(dst_ref, x_ref, n_rows, gain, shift, *, dst_offset=0):
    rows = BF16_SUBLANES
    for r in range(0, n_rows, rows):
        xx = x_ref[r:r + rows, :]
        inv = lax.rsqrt(jnp.mean(xx * xx, axis=-1, keepdims=True) + EPS)
        dst_ref[dst_offset + r:dst_offset + r + rows, :] = (xx * inv * gain + shift).astype(BF16)


def _resident(shape, index_map):
    return pl.BlockSpec(shape, index_map, pipeline_mode=pl.Buffered(1))


def _mod_kernel(c_ref, w_ref, b_ref, o_ref):
    c = c_ref[...]
    s = _silu(c).astype(BF16)
    o_ref[...] = jnp.dot(s, w_ref[...].astype(BF16), preferred_element_type=F32) + b_ref[...]


def _modulation(cc, w_ada, b_ada):
    depth, d, n = w_ada.shape
    tn = 1024
    return pl.pallas_call(
        _mod_kernel,
        grid=(depth, n // tn),
        in_specs=[
            pl.BlockSpec((MOD_ROWS, d), lambda l, j: (0, 0)),
            pl.BlockSpec((None, d, tn), lambda l, j: (l, 0, j)),
            pl.BlockSpec((None, 1, tn), lambda l, j: (l, 0, j)),
        ],
        out_specs=pl.BlockSpec((None, MOD_ROWS, tn), lambda l, j: (l, 0, j)),
        out_shape=jax.ShapeDtypeStruct((depth, MOD_ROWS, n), F32),
        compiler_params=_cparams(2),
        name="modulation",
    )(cc, w_ada, b_ada.reshape(depth, 1, n))


def _in_kernel(*refs, tm, tiles, rope):
    n_in = 9 if rope else 7
    x_ref, mod_ref, g1_ref, w_ref, qg_ref, kg_ref, sg_ref = refs[:7]
    cos_ref, sin_ref = refs[7:9] if rope else (None, None)
    out_refs = refs[n_in:-1]
    hs_ref = refs[-1]
    names = [nm for nm, ids in (("q", (0, 1)), ("kv", (2,)), ("pv", (3,)), ("u", (4,)), ("v", (5,)))
             if any(t in tiles for t in ids)]
    outs = dict(zip(names, out_refs))
    grp = 128

    gain = g1_ref[...] * (1.0 + mod_ref[1:2, :])
    _norm_rows(hs_ref, x_ref, tm, gain, mod_ref[0:1, :])
    hs = hs_ref[...]

    def project(i):
        return jnp.dot(hs, w_ref[:, i * IN_TILE:(i + 1) * IN_TILE], preferred_element_type=F32)

    def head(acc, r, h, g):
        xh = _rms(acc[r:r + grp, h * HEAD_DIM:(h + 1) * HEAD_DIM], g)
        if rope:
            xh = xh * cos_ref[r:r + grp, :] + pltpu.roll(xh, ROT_AXIS_DIM, 1) * sin_ref[r:r + grp, :]
        return xh.astype(BF16)

    def epilogue(t, acc):
        if t in (0, 1):
            qg = qg_ref[...] * (HEAD_DIM ** -0.5)
            for r in range(0, tm, grp):
                for h in range(IN_TILE // HEAD_DIM):
                    c0 = t * IN_TILE + h * HEAD_DIM
                    outs["q"][r:r + grp, c0:c0 + HEAD_DIM] = head(acc, r, h, qg)
        elif t == 2:
            kg = kg_ref[...]
            for r in range(0, tm, grp):
                for h in range(N_KV_HEADS):
                    outs["kv"][r:r + grp, h * HEAD_DIM:(h + 1) * HEAD_DIM] = head(acc, r, h, kg)
                outs["kv"][r:r + grp, KV_WIDTH:] = acc[r:r + grp, KV_WIDTH:].astype(BF16)
        elif t == 3:
            outs["pv"][...] = acc
        elif t == 4:
            for r in range(0, tm, grp):
                outs["u"][r:r + grp, :] = _gelu(acc[r:r + grp, :])
        else:
            sg = sg_ref[...]
            for r in range(0, tm, grp):
                outs["v"][r:r + grp, :] = _rms(_gelu(acc[r:r + grp, :]), sg).astype(BF16)

    nxt = project(0)
    for i, t in enumerate(tiles):
        acc = nxt
        if i + 1 < len(tiles):
            nxt = project(i + 1)
        epilogue(t, acc)


def _in_proj(x2, mod4, layer, mod_row, g1, w_in, qg, kg, sg, rope_tabs, *, seq, tm, tiles):
    rows, d = x2.shape
    assert rows % tm == 0 and seq % tm == 0
    tiles = tuple(tiles)
    assert tiles == tuple(range(tiles[0], tiles[0] + len(tiles))) and tiles[0] % len(tiles) == 0
    width = len(tiles) * IN_TILE
    wblk = tiles[0] // len(tiles)
    rope = rope_tabs is not None
    per_seq = seq // tm

    def mrow(m):
        return mod_row if mod_row is not None else m // per_seq

    in_specs = [
        pl.BlockSpec((tm, d), lambda m: (m, 0)),
        pl.BlockSpec((None, None, N_MOD, d), lambda m: (layer, mrow(m), 0, 0)),
        pl.BlockSpec((None, 1, d), lambda m: (layer, 0, 0)),
        _resident((None, d, width), lambda m: (layer, 0, wblk)),
        pl.BlockSpec((None, 1, HEAD_DIM), lambda m: (layer, 0, 0)),
        pl.BlockSpec((None, 1, HEAD_DIM), lambda m: (layer, 0, 0)),
        pl.BlockSpec((None, 1, SGU_WIDTH), lambda m: (layer, 0, 0)),
    ]
    args = [x2, mod4, g1, w_in, qg, kg, sg]
    if rope:
        in_specs += [pl.BlockSpec((tm, HEAD_DIM), lambda m: (m % per_seq, 0)) for _ in range(2)]
        args += list(rope_tabs)
    out_specs, out_shape = [], []

    def add(width, dtype):
        out_specs.append(pl.BlockSpec((tm, width), lambda m: (m, 0)))
        out_shape.append(jax.ShapeDtypeStruct((rows, width), dtype))

    if 0 in tiles or 1 in tiles:
        assert 0 in tiles and 1 in tiles
        add(ATTN_WIDTH, BF16)
    if 2 in tiles:
        add(IN_TILE, BF16)
    if 3 in tiles:
        add(POOL_WIDTH, F32)
    if 4 in tiles:
        add(SGU_WIDTH, F32)
    if 5 in tiles:
        add(SGU_WIDTH, BF16)
    return pl.pallas_call(
        functools.partial(_in_kernel, tm=tm, tiles=tiles, rope=rope),
        grid=(rows // tm,),
        in_specs=in_specs,
        out_specs=out_specs,
        out_shape=out_shape,
        scratch_shapes=[pltpu.VMEM((tm, d), BF16)],
        compiler_params=_cparams(1),
        name="in_proj",
    )(*args)


def _attn_unit(sink_ref, q_ref, kv_ref, kvc_ref, o_ref, qrows, kh, start, bias):
    rows = GQA_GROUP * BLOCK
    nband = 3 * BLOCK
    nt = (((1,), (1,)), ((), ()))
    q0 = kh * GQA_GROUP * HEAD_DIM
    kcol = slice(kh * HEAD_DIM, (kh + 1) * HEAD_DIM)
    vcol = slice(KV_WIDTH + kh * HEAD_DIM, KV_WIDTH + (kh + 1) * HEAD_DIM)
    qs = jnp.concatenate(
        [q_ref[qrows, q0 + g * HEAD_DIM:q0 + (g + 1) * HEAD_DIM] for g in range(GQA_GROUP)], axis=0)
    s_ctx = lax.dot_general(qs, kvc_ref[:, kcol], nt, preferred_element_type=F32)
    row = lax.broadcasted_iota(jnp.int32, (rows, 1), 0)
    sink = jnp.zeros((rows, 1), F32)
    for g in range(GQA_GROUP):
        sink = jnp.where(row // BLOCK == g, sink_ref[kh * GQA_GROUP + g], sink)
    m = jnp.maximum(jnp.max(s_ctx, axis=-1, keepdims=True), sink)
    if kv_ref is not None:
        s_band = lax.dot_general(qs, kv_ref[pl.ds(start, nband), kcol], nt,
                                 preferred_element_type=F32) + bias
        m = jnp.maximum(m, jnp.max(s_band, axis=-1, keepdims=True))
    p_ctx = jnp.exp(s_ctx - m)
    den = jnp.sum(p_ctx, axis=-1, keepdims=True) + jnp.exp(sink - m)
    o = jnp.dot(p_ctx.astype(BF16), kvc_ref[:, vcol], preferred_element_type=F32)
    if kv_ref is not None:
        p_band = jnp.exp(s_band - m)
        den = den + jnp.sum(p_band, axis=-1, keepdims=True)
        o = o + jnp.dot(p_band.astype(BF16), kv_ref[pl.ds(start, nband), vcol],
                        preferred_element_type=F32)
    o = o / den
    for g in range(GQA_GROUP):
        o_ref[qrows, q0 + g * HEAD_DIM:q0 + (g + 1) * HEAD_DIM] = (
            o[g * BLOCK:(g + 1) * BLOCK, :].astype(BF16))


def _attn_kernel(*refs, seq, band, qblocks):
    if band:
        sink_ref, q_ref, kv_ref, kvc_ref, o_ref = refs
    else:
        sink_ref, q_ref, kvc_ref, o_ref = refs
        kv_ref = None
    for j in range(qblocks):
        n = pl.program_id(1) * qblocks + j
        qrows = slice(j * BLOCK, (j + 1) * BLOCK)
        start = bias = None
        if band:
            nband = 3 * BLOCK
            start = pl.multiple_of(jnp.clip((n - 1) * BLOCK, 0, seq - nband), BLOCK)
            dist = (n * BLOCK - start) + (lax.broadcasted_iota(jnp.int32, (BLOCK, nband), 0)
                                          - lax.broadcasted_iota(jnp.int32, (BLOCK, nband), 1))
            bias = jnp.where(jnp.abs(dist) <= WINDOW, 0.0, -jnp.inf).astype(F32)
            bias = jnp.concatenate([bias] * GQA_GROUP, axis=0)
        for kh in range(N_KV_HEADS):
            _attn_unit(sink_ref, q_ref, kv_ref, kvc_ref, o_ref, qrows, kh, start, bias)


def _attention(sink, q, kv, kvc, *, batch, seq, ctx_len, band, qblocks):
    tq = qblocks * BLOCK
    assert seq % tq == 0
    steps = seq // tq
    in_specs = [
        pl.BlockSpec(memory_space=pltpu.SMEM),
        pl.BlockSpec((tq, ATTN_WIDTH), lambda b, n: (b * steps + n, 0)),
    ]
    args = [sink, q]
    if band:
        in_specs.append(pl.BlockSpec((seq, 2 * KV_WIDTH), lambda b, n: (b, 0)))
        args.append(kv)
    in_specs.append(pl.BlockSpec((ctx_len, 2 * KV_WIDTH), lambda b, n: (b, 0)))
    args.append(kvc)
    return pl.pallas_call(
        functools.partial(_attn_kernel, seq=seq, band=band, qblocks=qblocks),
        grid=(batch, steps),
        in_specs=in_specs,
        out_specs=pl.BlockSpec((tq, ATTN_WIDTH), lambda b, n: (b * steps + n, 0)),
        out_shape=jax.ShapeDtypeStruct((batch * seq, ATTN_WIDTH), BF16),
        compiler_params=_cparams(2),
        name="attention",
    )(*args)


def _out_kernel(attn_ref, pv_ref, pvp_ref, pvn_ref, u_ref, v_ref, x_ref, mod_ref, pw_ref, ps_ref,
                sw_ref, sbt_ref, wo_ref, o_ref, pbuf_ref, mix_ref, *, tm, seq):
    m = pl.program_id(0)
    pos0 = (m * tm) % seq
    o_ref[...] = jnp.dot(attn_ref[...], wo_ref[0:ATTN_WIDTH, :], preferred_element_type=F32)
    pbuf_ref[0:SUBLANES, :] = jnp.where(pos0 == 0, 0.0, pvp_ref[...])
    pbuf_ref[SUBLANES:SUBLANES + tm, :] = pv_ref[...]
    pbuf_ref[SUBLANES + tm:, :] = jnp.where(pos0 + tm == seq, 0.0, pvn_ref[...])
    pos = pos0 + lax.broadcasted_iota(jnp.int32, (tm, 1), 0)
    ext = tm + 2 * SUBLANES

    def ahead(a, k):
        return pltpu.roll(a, ext - k, 0)

    for gi, w in enumerate(POOL_WINDOWS):
        c0 = gi * POOL_GROUP_DIM
        cols = slice(c0, c0 + POOL_GROUP_DIM)
        run, span = pbuf_ref[:, cols], 1
        while 2 * span < w:
            run = run + ahead(run, span)
            span *= 2
        first = run[0:tm] if w // 2 == SUBLANES else ahead(run, SUBLANES - w // 2)[0:tm]
        acc = first + run[SUBLANES:SUBLANES + tm]
        lo = jnp.maximum(pos - w // 2, 0)
        hi = jnp.minimum(pos - w // 2 + w, seq)
        y = (acc * (1.0 / (hi - lo).astype(F32)) - pv_ref[:, cols]).astype(BF16)
        z = jnp.dot(y, pw_ref[gi], preferred_element_type=F32) * ps_ref[:, cols]
        mix_ref[:, cols] = z.astype(BF16)
    n_chunks = tm // SGU_CHUNK
    for h in range(N_SGU_HEADS):
        cols = slice(h * LANES, (h + 1) * LANES)
        vh = jnp.concatenate(
            [v_ref[ci * SGU_CHUNK:(ci + 1) * SGU_CHUNK, cols] for ci in range(n_chunks)], axis=1)
        r = jnp.dot(sw_ref[h], vh, preferred_element_type=F32) + sbt_ref[:, h:h + 1]
        for ci in range(n_chunks):
            rs = slice(ci * SGU_CHUNK, (ci + 1) * SGU_CHUNK)
            gated = u_ref[rs, cols] * r[:, ci * SGU_CHUNK:(ci + 1) * SGU_CHUNK]
            mix_ref[rs, POOL_WIDTH + h * LANES:POOL_WIDTH + (h + 1) * LANES] = gated.astype(BF16)
    res = o_ref[...] + jnp.dot(mix_ref[...], wo_ref[ATTN_WIDTH:, :], preferred_element_type=F32)
    o_ref[...] = x_ref[...] + mod_ref[2:3, :] * res


def _out_proj(attn, pv, u, v, x2, mod4, layer, mod_row, pool_w, pool_scale, sgu_w, sgu_bt, w_out,
              *, seq, tm):
    rows, d = x2.shape
    assert rows % tm == 0 and seq % tm == 0 and tm % SGU_CHUNK == 0
    per_seq = seq // tm
    hb = tm // SUBLANES
    last_hb = rows // SUBLANES - 1

    def mrow(m):
        return mod_row if mod_row is not None else m // per_seq

    def layer_block(a):
        shape = a.shape[1:]
        return pl.BlockSpec((None,) + shape, lambda m: (layer,) + (0,) * len(shape))

    return pl.pallas_call(
        functools.partial(_out_kernel, tm=tm, seq=seq),
        grid=(rows // tm,),
        in_specs=[
            pl.BlockSpec((tm, ATTN_WIDTH), lambda m: (m, 0)),
            pl.BlockSpec((tm, POOL_WIDTH), lambda m: (m, 0)),
            pl.BlockSpec((SUBLANES, POOL_WIDTH), lambda m: (jnp.maximum(m * hb - 1, 0), 0)),
            pl.BlockSpec((SUBLANES, POOL_WIDTH), lambda m: (jnp.minimum((m + 1) * hb, last_hb), 0)),
            pl.BlockSpec((tm, SGU_WIDTH), lambda m: (m, 0)),
            pl.BlockSpec((tm, SGU_WIDTH), lambda m: (m, 0)),
            pl.BlockSpec((tm, d), lambda m: (m, 0)),
            pl.BlockSpec((None, None, N_MOD, d), lambda m: (layer, mrow(m), 0, 0)),
            layer_block(pool_w),
            layer_block(pool_scale),
            layer_block(sgu_w),
            layer_block(sgu_bt),
            _resident((None,) + w_out.shape[1:], lambda m: (layer, 0, 0)),
        ],
        out_specs=pl.BlockSpec((tm, d), lambda m: (m, 0)),
        out_shape=jax.ShapeDtypeStruct((rows, d), F32),
        scratch_shapes=[pltpu.VMEM((tm + 2 * SUBLANES, POOL_WIDTH), F32),
                        pltpu.VMEM((tm, POOL_WIDTH + SGU_WIDTH), BF16)],
        compiler_params=_cparams(1),
        name="out_proj",
    )(attn, pv, pv, pv, u, v, x2, mod4, pool_w, pool_scale, sgu_w, sgu_bt, w_out)


def _ffn_kernel(x_ref, xp_ref, xn_ref, mod_ref, g2_ref, wg_ref, wv_ref, cwg_ref, cwv_ref, cbg_ref,
                cbv_ref, wd_ref, fg_ref, o_ref, hs_ref, ag_ref, av_ref, act_ref, *, tm, seq, final):
    m = pl.program_id(0)
    f = pl.program_id(1)
    rows = BF16_SUBLANES
    grp = 128

    @pl.when(f == 0)
    def _():
        shift = mod_ref[3:4, :]
        gain = g2_ref[...] * (1.0 + mod_ref[4:5, :])
        _norm_rows(hs_ref, x_ref, tm, gain, shift, dst_offset=HALO)

        def halo(ref):
            xx = ref[...]
            inv = lax.rsqrt(jnp.mean(xx * xx, axis=-1, keepdims=True) + EPS)
            return xx * inv * gain + shift

        pos0 = (m * tm) % seq
        hn = jnp.where((pos0 + tm) % seq == 0, 0.0, halo(xn_ref))
        hp = jnp.where(pos0 == 0, 0.0, halo(xp_ref))
        pad = jnp.zeros_like(hp)
        hs_ref[0:HALO, :] = jnp.concatenate([pad, hp], axis=0).astype(BF16)
        hs_ref[HALO + tm:, :] = jnp.concatenate([hn, pad], axis=0).astype(BF16)
        o_ref[...] = jnp.zeros(o_ref.shape, F32)

    hs = hs_ref[...]
    part = FF_TILE // FF_PARTS
    parts = [slice(p * part, (p + 1) * part) for p in range(FF_PARTS)]
    for cs in parts:
        ag_ref[:, cs] = jnp.dot(hs, wg_ref[:, cs], preferred_element_type=F32)
        av_ref[:, cs] = jnp.dot(hs, wv_ref[:, cs], preferred_element_type=F32)

    row_in_grp = lax.broadcasted_iota(jnp.int32, (grp, 1), 0)

    def conv(a_ref, cw_ref, cb_ref, r, cs):
        prev = a_ref[HALO - 1 + r:HALO - 1 + r + grp, cs]
        cur = a_ref[HALO + r:HALO + r + grp, cs]
        nxt = a_ref[HALO + 1 + r:HALO + 1 + r + grp, cs]
        if r > 0 and r % seq == 0:
            prev = jnp.where(row_in_grp == 0, 0.0, prev)
        if r + grp < tm and (r + grp) % seq == 0:
            nxt = jnp.where(row_in_grp == grp - 1, 0.0, nxt)
        return cb_ref[:, cs] + prev * cw_ref[0:1, cs] + cur * cw_ref[1:2, cs] + nxt * cw_ref[2:3, cs]

    for cs in parts:
        for r in range(0, tm, grp):
            act = _silu(conv(ag_ref, cwg_ref, cbg_ref, r, cs)) * conv(av_ref, cwv_ref, cbv_ref, r, cs)
            act_ref[r:r + grp, cs] = act.astype(BF16)
    for r in range(0, tm, FF_DOWN_ROWS):
        rs = slice(r, r + FF_DOWN_ROWS)
        o_ref[rs, :] += jnp.dot(act_ref[rs, :], wd_ref[...], preferred_element_type=F32)

    @pl.when(f == pl.num_programs(1) - 1)
    def _():
        gate2 = mod_ref[5:6, :]
        fg = fg_ref[...]

        for r in range(0, tm, rows):
            y = x_ref[r:r + rows, :] + gate2 * o_ref[r:r + rows, :]
            if final:
                y = _rms(y, fg)
            o_ref[r:r + rows, :] = y


def _ffn(x2, mod4, layer, mod_row, g2, w_up, conv_w, conv_b, w_down, final_g, *, seq, tm, final):
    rows, d = x2.shape
    assert rows % tm == 0 and (seq % tm == 0 or tm % seq == 0) and seq % LANES == 0
    assert mod_row is not None or seq % tm == 0
    per_seq = max(seq // tm, 1)
    hb = tm // SUBLANES
    last_hb = rows // SUBLANES - 1
    nf = D_FF // FF_TILE

    def mrow(m):
        return mod_row if mod_row is not None else m // per_seq

    return pl.pallas_call(
        functools.partial(_ffn_kernel, tm=tm, seq=seq, final=final),
        grid=(rows // tm, nf),
        in_specs=[
            pl.BlockSpec((tm, d), lambda m, f: (m, 0), pipeline_mode=pl.Buffered(1)),
            pl.BlockSpec((SUBLANES, d), lambda m, f: (jnp.maximum(m * hb - 1, 0), 0)),
            pl.BlockSpec((SUBLANES, d), lambda m, f: (jnp.minimum((m + 1) * hb, last_hb), 0)),
            pl.BlockSpec((None, None, N_MOD, d), lambda m, f: (layer, mrow(m), 0, 0)),
            pl.BlockSpec((None, 1, d), lambda m, f: (layer, 0, 0)),
            pl.BlockSpec((None, d, FF_TILE), lambda m, f: (layer, 0, f)),
            pl.BlockSpec((None, d, FF_TILE), lambda m, f: (layer, 0, f + nf)),
            pl.BlockSpec((None, 3, FF_TILE), lambda m, f: (layer, 0, f)),
            pl.BlockSpec((None, 3, FF_TILE), lambda m, f: (layer, 0, f + nf)),
            pl.BlockSpec((None, 1, FF_TILE), lambda m, f: (layer, 0, f)),
            pl.BlockSpec((None, 1, FF_TILE), lambda m, f: (layer, 0, f + nf)),
            pl.BlockSpec((None, FF_TILE, d), lambda m, f: (layer, f, 0)),
            pl.BlockSpec((1, d), lambda m, f: (0, 0)),
        ],
        out_specs=pl.BlockSpec((tm, d), lambda m, f: (m, 0)),
        out_shape=jax.ShapeDtypeStruct((rows, d), F32),
        scratch_shapes=[pltpu.VMEM((tm + 2 * HALO, d), BF16),
                        pltpu.VMEM((tm + 2 * HALO, FF_TILE), F32),
                        pltpu.VMEM((tm + 2 * HALO, FF_TILE), F32),
                        pltpu.VMEM((tm, FF_TILE), BF16)],
        compiler_params=_cparams(2),
        name="ffn",
    )(x2, x2, x2, mod4, g2, w_up, w_up, conv_w, conv_w, conv_b, conv_b, w_down, final_g)


def _head_lane_order(a, n_heads):
    lead = a.shape[:-1]
    a = a.reshape(lead + (n_heads, 2, 2, ROT_PAIR))
    return jnp.swapaxes(a, -2, -3).reshape(lead + (n_heads * HEAD_DIM,))


def _rope_tables(seq):
    rows = seq // GRID_W
    row_ids = jnp.repeat(jnp.arange(rows), GRID_W).astype(F32)
    col_ids = jnp.tile(jnp.arange(GRID_W), rows).astype(F32)
    inv = 1.0 / (ROPE_BASE ** (jnp.arange(0, ROT_AXIS_DIM, 2, dtype=F32) / ROT_AXIS_DIM))
    ang_r = row_ids[:, None] * inv
    ang_c = col_ids[:, None] * inv
    cr, sr, cc, sc = jnp.cos(ang_r), jnp.sin(ang_r), jnp.cos(ang_c), jnp.sin(ang_c)
    cos = jnp.concatenate([cr, cc, cr, cc], axis=-1)
    sin_signed = jnp.concatenate([-sr, -sc, sr, sc], axis=-1)
    return cos, sin_signed


def kernel(x, c, ctx, c_ctx, norm1_g, norm2_g, w_ada, b_ada, w_in, q_norm_g, k_norm_g, attn_sink,
           pool_w, pool_scale, sgu_norm_g, sgu_w, sgu_b, w_out, w_up, conv_w, conv_b, w_down,
           final_norm_g):
    batch, seq, d = x.shape
    ctx_len = ctx.shape[1]
    depth = w_ada.shape[0]
    assert d == D_MODEL and batch + 1 <= MOD_ROWS
    ctx_row = batch

    cc = jnp.zeros((MOD_ROWS, d), F32).at[:batch].set(c).at[ctx_row].set(c_ctx)
    mod4 = _modulation(cc, w_ada, b_ada).reshape(depth, MOD_ROWS, N_MOD, d)
    rope_tabs = _rope_tables(seq)

    qk = ATTN_WIDTH + KV_WIDTH
    w_in_b = jnp.concatenate(
        [_head_lane_order(w_in[..., :qk], N_Q_HEADS + N_KV_HEADS), w_in[..., qk:]], axis=-1
    ).astype(BF16)
    w_out_b, w_up_b, w_down_b = (w.astype(BF16) for w in (w_out, w_up, w_down))
    pool_w_b, sgu_w_b = pool_w.astype(BF16), sgu_w.astype(BF16)
    g1, g2 = norm1_g[:, None], norm2_g[:, None]
    qg = _head_lane_order(q_norm_g, 1)[:, None]
    kg = _head_lane_order(k_norm_g, 1)[:, None]
    sg = sgu_norm_g[:, None]
    mixer = (pool_w_b, pool_scale[:, None], sgu_w_b, jnp.swapaxes(sgu_b, 1, 2), w_out_b)
    mlp = (g2, w_up_b, conv_w, conv_b[:, None], w_down_b, final_norm_g[None])

    x2 = x.reshape(batch * seq, d)
    xc2 = ctx.reshape(batch * ctx_len, d)
    lat_tm = 512
    ctx_tm = ctx_len
    ffn_tm = 1024
    attn_qblocks = 2
    for l in range(depth):
        last = l == depth - 1
        q, kv, pv, u, v = _in_proj(x2, mod4, l, None, g1, w_in_b, qg, kg, sg, rope_tabs,
                                   seq=seq, tm=lat_tm, tiles=range(6))
        if last:
            (kvc,) = _in_proj(xc2, mod4, l, ctx_row, g1, w_in_b, qg, kg, sg, None,
                              seq=ctx_len, tm=ctx_tm, tiles=(2,))
        else:
            qc, kvc, pvc, uc, vc = _in_proj(xc2, mod4, l, ctx_row, g1, w_in_b, qg, kg, sg, None,
                                            seq=ctx_len, tm=ctx_tm, tiles=range(6))
        attn = _attention(attn_sink[l], q, kv, kvc, batch=batch, seq=seq, ctx_len=ctx_len, band=True,
                          qblocks=attn_qblocks)
        x2 = _out_proj(attn, pv, u, v, x2, mod4, l, None, *mixer, seq=seq, tm=lat_tm)
        if not last:
            attn_c = _attention(attn_sink[l], qc, None, kvc, batch=batch, seq=ctx_len,
                                ctx_len=ctx_len, band=False, qblocks=attn_qblocks)
            xc2 = _out_proj(attn_c, pvc, uc, vc, xc2, mod4, l, ctx_row, *mixer, seq=ctx_len, tm=ctx_tm)
        x2 = _ffn(x2, mod4, l, None, *mlp, seq=seq, tm=ffn_tm, final=last)
        if not last:
            xc2 = _ffn(xc2, mod4, l, ctx_row, *mlp, seq=ctx_len, tm=ffn_tm, final=False)
    return x2.reshape(batch, seq, d)
```

```python
import functools

import jax
import jax.numpy as jnp
from jax import lax
from jax.experimental import pallas as pl
from jax.experimental.pallas import tpu as pltpu

F32 = jnp.float32
BF16 = jnp.bfloat16

D_MODEL = 2048
GRID_W = 64
HEAD_DIM = 128
N_Q_HEADS = 8
N_KV_HEADS = 2
GQA_GROUP = N_Q_HEADS // N_KV_HEADS
ATTN_WIDTH = N_Q_HEADS * HEAD_DIM
KV_WIDTH = N_KV_HEADS * HEAD_DIM
WINDOW = 128
BLOCK = 128
ROPE_BASE = 10000.0
ROT_AXIS_DIM = HEAD_DIM // 2
ROT_PAIR = ROT_AXIS_DIM // 2
POOL_WINDOWS = (2, 4, 8, 16)
POOL_WIDTH = 512
POOL_GROUP_DIM = 128
SGU_WIDTH = 512
N_SGU_HEADS = 4
SGU_CHUNK = 128
IN_WIDTH = 3072
D_FF = 5632
N_MOD = 6
EPS = 1e-6

SUBLANES = 8
BF16_SUBLANES = 16
LANES = 128
HALO = BF16_SUBLANES
MOD_ROWS = 16
IN_TILE = 512
FF_TILE = 512
FF_PARTS = 2
FF_DOWN_ROWS = 256
VMEM_LIMIT = 60 * 1024 * 1024


def _cparams(n_axes):
    return pltpu.CompilerParams(
        dimension_semantics=("arbitrary",) * n_axes, vmem_limit_bytes=VMEM_LIMIT)


def _rms(x, g):
    return x * lax.rsqrt(jnp.mean(x * x, axis=-1, keepdims=True) + EPS) * g


def _gelu(x):
    return 0.5 * x * (1.0 + lax.erf(x * 0.7071067811865476))


def _silu(x):
    return x * jax.nn.sigmoid(x)


def _norm_rows(dst_ref, x_ref, n_rows, gain, shift, *, dst_offset=0):
    rows = BF16_SUBLANES
    for r in range(0, n_rows, rows):
        xx = x_ref[r:r + rows, :]
        inv = lax.rsqrt(jnp.mean(xx * xx, axis=-1, keepdims=True) + EPS)
        dst_ref[dst_offset + r:dst_offset + r + rows, :] = (xx * inv * gain + shift).astype(BF16)


def _resident(shape, index_map):
    return pl.BlockSpec(shape, index_map, pipeline_mode=pl.Buffered(1))


def _mod_kernel(c_ref, w_ref, b_ref, o_ref):
    c = c_ref[...]
    s = _silu(c).astype(BF16)
    o_ref[...] = jnp.dot(s, w_ref[...].astype(BF16), preferred_element_type=F32) + b_ref[...]


def _modulation(cc, w_ada, b_ada):
    depth, d, n = w_ada.shape
    tn = 1024
    return pl.pallas_call(
        _mod_kernel,
        grid=(depth, n // tn),
        in_specs=[
            pl.BlockSpec((MOD_ROWS, d), lambda l, j: (0, 0)),
            pl.BlockSpec((None, d, tn), lambda l, j: (l, 0, j)),
            pl.BlockSpec((None, 1, tn), lambda l, j: (l, 0, j)),
        ],
        out_specs=pl.BlockSpec((None, MOD_ROWS, tn), lambda l, j: (l, 0, j)),
        out_shape=jax.ShapeDtypeStruct((depth, MOD_ROWS, n), F32),
        compiler_params=_cparams(2),
        name="modulation",
    )(cc, w_ada, b_ada.reshape(depth, 1, n))


def _in_kernel(*refs, tm, tiles, rope):
    n_in = 9 if rope else 7
    x_ref, mod_ref, g1_ref, w_ref, qg_ref, kg_ref, sg_ref = refs[:7]
    cos_ref, sin_ref = refs[7:9] if rope else (None, None)
    out_refs = refs[n_in:-1]
    hs_ref = refs[-1]
    names = [nm for nm, ids in (("q", (0, 1)), ("kv", (2,)), ("pv", (3,)), ("u", (4,)), ("v", (5,)))
             if any(t in tiles for t in ids)]
    outs = dict(zip(names, out_refs))
    grp = 128

    gain = g1_ref[...] * (1.0 + mod_ref[1:2, :])
    _norm_rows(hs_ref, x_ref, tm, gain, mod_ref[0:1, :])
    hs = hs_ref[...]

    def project(i):
        return jnp.dot(hs, w_ref[:, i * IN_TILE:(i + 1) * IN_TILE], preferred_element_type=F32)

    def head(acc, r, h, g):
        xh = _rms(acc[r:r + grp, h * HEAD_DIM:(h + 1) * HEAD_DIM], g)
        if rope:
            xh = xh * cos_ref[r:r + grp, :] + pltpu.roll(xh, ROT_AXIS_DIM, 1) * sin_ref[r:r + grp, :]
        return xh.astype(BF16)

    def epilogue(t, acc):
        if t in (0, 1):
            qg = qg_ref[...] * (HEAD_DIM ** -0.5)
            for r in range(0, tm, grp):
                for h in range(IN_TILE // HEAD_DIM):
                    c0 = t * IN_TILE + h * HEAD_DIM
                    outs["q"][r:r + grp, c0:c0 + HEAD_DIM] = head(acc, r, h, qg)
        elif t == 2:
            kg = kg_ref[...]
            for r in range(0, tm, grp):
                for h in range(N_KV_HEADS):
                    outs["kv"][r:r + grp, h * HEAD_DIM:(h + 1) * HEAD_DIM] = head(acc, r, h, kg)
                outs["kv"][r:r + grp, KV_WIDTH:] = acc[r:r + grp, KV_WIDTH:].astype(BF16)
        elif t == 3:
            outs["pv"][...] = acc
        elif t == 4:
            for r in range(0, tm, grp):
                outs["u"][r:r + grp, :] = _gelu(acc[r:r + grp, :])
        else:
            sg = sg_ref[...]
            for r in range(0, tm, grp):
                outs["v"][r:r + grp, :] = _rms(_gelu(acc[r:r + grp, :]), sg).astype(BF16)

    nxt = project(0)
    for i, t in enumerate(tiles):
        acc = nxt
        if i + 1 < len(tiles):
            nxt = project(i + 1)
        epilogue(t, acc)


def _in_proj(x2, mod4, layer, mod_row, g1, w_in, qg, kg, sg, rope_tabs, *, seq, tm, tiles):
    rows, d = x2.shape
    assert rows % tm == 0 and seq % tm == 0
    tiles = tuple(tiles)
    assert tiles == tuple(range(tiles[0], tiles[0] + len(tiles))) and tiles[0] % len(tiles) == 0
    width = len(tiles) * IN_TILE
    wblk = tiles[0] // len(tiles)
    rope = rope_tabs is not None
    per_seq = seq // tm

    def mrow(m):
        return mod_row if mod_row is not None else m // per_seq

    in_specs = [
        pl.BlockSpec((tm, d), lambda m: (m, 0)),
        pl.BlockSpec((None, None, N_MOD, d), lambda m: (layer, mrow(m), 0, 0)),
        pl.BlockSpec((None, 1, d), lambda m: (layer, 0, 0)),
        _resident((None, d, width), lambda m: (layer, 0, wblk)),
        pl.BlockSpec((None, 1, HEAD_DIM), lambda m: (layer, 0, 0)),
        pl.BlockSpec((None, 1, HEAD_DIM), lambda m: (layer, 0, 0)),
        pl.BlockSpec((None, 1, SGU_WIDTH), lambda m: (layer, 0, 0)),
    ]
    args = [x2, mod4, g1, w_in, qg, kg, sg]
    if rope:
        in_specs += [pl.BlockSpec((tm, HEAD_DIM), lambda m: (m % per_seq, 0)) for _ in range(2)]
        args += list(rope_tabs)
    out_specs, out_shape = [], []

    def add(width, dtype):
        out_specs.append(pl.BlockSpec((tm, width), lambda m: (m, 0)))
        out_shape.append(jax.ShapeDtypeStruct((rows, width), dtype))

    if 0 in tiles or 1 in tiles:
        assert 0 in tiles and 1 in tiles
        add(ATTN_WIDTH, BF16)
    if 2 in tiles:
        add(IN_TILE, BF16)
    if 3 in tiles:
        add(POOL_WIDTH, F32)
    if 4 in tiles:
        add(SGU_WIDTH, F32)
    if 5 in tiles:
        add(SGU_WIDTH, BF16)
    return pl.pallas_call(
        functools.partial(_in_kernel, tm=tm, tiles=tiles, rope=rope),
        grid=(rows // tm,),
        in_specs=in_specs,
        out_specs=out_specs,
        out_shape=out_shape,
        scratch_shapes=[pltpu.VMEM((tm, d), BF16)],
        compiler_params=_cparams(1),
        name="in_proj",
    )(*args)


def _attn_unit(sink_ref, q_ref, kv_ref, kvc_ref, o_ref, qrows, kh, start, bias):
    rows = GQA_GROUP * BLOCK
    nband = 3 * BLOCK
    nt = (((1,), (1,)), ((), ()))
    q0 = kh * GQA_GROUP * HEAD_DIM
    kcol = slice(kh * HEAD_DIM, (kh + 1) * HEAD_DIM)
    vcol = slice(KV_WIDTH + kh * HEAD_DIM, KV_WIDTH + (kh + 1) * HEAD_DIM)
    qs = jnp.concatenate(
        [q_ref[qrows, q0 + g * HEAD_DIM:q0 + (g + 1) * HEAD_DIM] for g in range(GQA_GROUP)], axis=0)
    s_ctx = lax.dot_general(qs, kvc_ref[:, kcol], nt, preferred_element_type=F32)
    row = lax.broadcasted_iota(jnp.int32, (rows, 1), 0)
    sink = jnp.zeros((rows, 1), F32)
    for g in range(GQA_GROUP):
        sink = jnp.where(row // BLOCK == g, sink_ref[kh * GQA_GROUP + g], sink)
    m = jnp.maximum(jnp.max(s_ctx, axis=-1, keepdims=True), sink)
    if kv_ref is not None:
        s_band = lax.dot_general(qs, kv_ref[pl.ds(start, nband), kcol], nt,
                                 preferred_element_type=F32) + bias
        m = jnp.maximum(m, jnp.max(s_band, axis=-1, keepdims=True))
    p_ctx = jnp.exp(s_ctx - m)
    den = jnp.sum(p_ctx, axis=-1, keepdims=True) + jnp.exp(sink - m)
    o = jnp.dot(p_ctx.astype(BF16), kvc_ref[:, vcol], preferred_element_type=F32)
    if kv_ref is not None:
        p_band = jnp.exp(s_band - m)
        den = den + jnp.sum(p_band, axis=-1, keepdims=True)
        o = o + jnp.dot(p_band.astype(BF16), kv_ref[pl.ds(start, nband), vcol],
                        preferred_element_type=F32)
    o = o / den
    for g in range(GQA_GROUP):
        o_ref[qrows, q0 + g * HEAD_DIM:q0 + (g + 1) * HEAD_DIM] = (
            o[g * BLOCK:(g + 1) * BLOCK, :].astype(BF16))


def _attn_kernel(*refs, seq, band, qblocks):
    if band:
        sink_ref, q_ref, kv_ref, kvc_ref, o_ref = refs
    else:
        sink_ref, q_ref, kvc_ref, o_ref = refs
        kv_ref = None
    for j in range(qblocks):
        n = pl.program_id(1) * qblocks + j
        qrows = slice(j * BLOCK, (j + 1) * BLOCK)
        start = bias = None
        if band:
            nband = 3 * BLOCK
            start = pl.multiple_of(jnp.clip((n - 1) * BLOCK, 0, seq - nband), BLOCK)
            dist = (n * BLOCK - start) + (lax.broadcasted_iota(jnp.int32, (BLOCK, nband), 0)
                                          - lax.broadcasted_iota(jnp.int32, (BLOCK, nband), 1))
            bias = jnp.where(jnp.abs(dist) <= WINDOW, 0.0, -jnp.inf).astype(F32)
            bias = jnp.concatenate([bias] * GQA_GROUP, axis=0)
        for kh in range(N_KV_HEADS):
            _attn_unit(sink_ref, q_ref, kv_ref, kvc_ref, o_ref, qrows, kh, start, bias)


def _attention(sink, q, kv, kvc, *, batch, seq, ctx_len, band, qblocks):
    tq = qblocks * BLOCK
    assert seq % tq == 0
    steps = seq // tq
    in_specs = [
        pl.BlockSpec(memory_space=pltpu.SMEM),
        pl.BlockSpec((tq, ATTN_WIDTH), lambda b, n: (b * steps + n, 0)),
    ]
    args = [sink, q]
    if band:
        in_specs.append(pl.BlockSpec((seq, 2 * KV_WIDTH), lambda b, n: (b, 0)))
        args.append(kv)
    in_specs.append(pl.BlockSpec((ctx_len, 2 * KV_WIDTH), lambda b, n: (b, 0)))
    args.append(kvc)
    return pl.pallas_call(
        functools.partial(_attn_kernel, seq=seq, band=band, qblocks=qblocks),
        grid=(batch, steps),
        in_specs=in_specs,
        out_specs=pl.BlockSpec((tq, ATTN_WIDTH), lambda b, n: (b * steps + n, 0)),
        out_shape=jax.ShapeDtypeStruct((batch * seq, ATTN_WIDTH), BF16),
        compiler_params=_cparams(2),
        name="attention",
    )(*args)


def _out_kernel(attn_ref, pv_ref, pvp_ref, pvn_ref, u_ref, v_ref, x_ref, mod_ref, pw_ref, ps_ref,
                sw_ref, sbt_ref, wo_ref, o_ref, pbuf_ref, mix_ref, *, tm, seq):
    m = pl.program_id(0)
    pos0 = (m * tm) % seq
    o_ref[...] = jnp.dot(attn_ref[...], wo_ref[0:ATTN_WIDTH, :], preferred_element_type=F32)
    pbuf_ref[0:SUBLANES, :] = jnp.where(pos0 == 0, 0.0, pvp_ref[...])
    pbuf_ref[SUBLANES:SUBLANES + tm, :] = pv_ref[...]
    pbuf_ref[SUBLANES + tm:, :] = jnp.where(pos0 + tm == seq, 0.0, pvn_ref[...])
    pos = pos0 + lax.broadcasted_iota(jnp.int32, (tm, 1), 0)
    ext = tm + 2 * SUBLANES

    def ahead(a, k):
        return pltpu.roll(a, ext - k, 0)

    for gi, w in enumerate(POOL_WINDOWS):
        c0 = gi * POOL_GROUP_DIM
        cols = slice(c0, c0 + POOL_GROUP_DIM)
        run, span = pbuf_ref[:, cols], 1
        while 2 * span < w:
            run = run + ahead(run, span)
            span *= 2
        first = run[0:tm] if w // 2 == SUBLANES else ahead(run, SUBLANES - w // 2)[0:tm]
        acc = first + run[SUBLANES:SUBLANES + tm]
        lo = jnp.maximum(pos - w // 2, 0)
        hi = jnp.minimum(pos - w // 2 + w, seq)
        y = (acc * (1.0 / (hi - lo).astype(F32)) - pv_ref[:, cols]).astype(BF16)
        z = jnp.dot(y, pw_ref[gi], preferred_element_type=F32) * ps_ref[:, cols]
        mix_ref[:, cols] = z.astype(BF16)
    n_chunks = tm // SGU_CHUNK
    for h in range(N_SGU_HEADS):
        cols = slice(h * LANES, (h + 1) * LANES)
        vh = jnp.concatenate(
            [v_ref[ci * SGU_CHUNK:(ci + 1) * SGU_CHUNK, cols] for ci in range(n_chunks)], axis=1)
        r = jnp.dot(sw_ref[h], vh, preferred_element_type=F32) + sbt_ref[:, h:h + 1]
        for ci in range(n_chunks):
            rs = slice(ci * SGU_CHUNK, (ci + 1) * SGU_CHUNK)
            gated = u_ref[rs, cols] * r[:, ci * SGU_CHUNK:(ci + 1) * SGU_CHUNK]
            mix_ref[rs, POOL_WIDTH + h * LANES:POOL_WIDTH + (h + 1) * LANES] = gated.astype(BF16)
    res = o_ref[...] + jnp.dot(mix_ref[...], wo_ref[ATTN_WIDTH:, :], preferred_element_type=F32)
    o_ref[...] = x_ref[...] + mod_ref[2:3, :] * res


def _out_proj(attn, pv, u, v, x2, mod4, layer, mod_row, pool_w, pool_scale, sgu_w, sgu_bt, w_out,
              *, seq, tm):
    rows, d = x2.shape
    assert rows % tm == 0 and seq % tm == 0 and tm % SGU_CHUNK == 0
    per_seq = seq // tm
    hb = tm // SUBLANES
    last_hb = rows // SUBLANES - 1

    def mrow(m):
        return mod_row if mod_row is not None else m // per_seq

    def layer_block(a):
        shape = a.shape[1:]
        return pl.BlockSpec((None,) + shape, lambda m: (layer,) + (0,) * len(shape))

    return pl.pallas_call(
        functools.partial(_out_kernel, tm=tm, seq=seq),
        grid=(rows // tm,),
        in_specs=[
            pl.BlockSpec((tm, ATTN_WIDTH), lambda m: (m, 0)),
            pl.BlockSpec((tm, POOL_WIDTH), lambda m: (m, 0)),
            pl.BlockSpec((SUBLANES, POOL_WIDTH), lambda m: (jnp.maximum(m * hb - 1, 0), 0)),
            pl.BlockSpec((SUBLANES, POOL_WIDTH), lambda m: (jnp.minimum((m + 1) * hb, last_hb), 0)),
            pl.BlockSpec((tm, SGU_WIDTH), lambda m: (m, 0)),
            pl.BlockSpec((tm, SGU_WIDTH), lambda m: (m, 0)),
            pl.BlockSpec((tm, d), lambda m: (m, 0)),
            pl.BlockSpec((None, None, N_MOD, d), lambda m: (layer, mrow(m), 0, 0)),
            layer_block(pool_w),
            layer_block(pool_scale),
            layer_block(sgu_w),
            layer_block(sgu_bt),
            _resident((None,) + w_out.shape[1:], lambda m: (layer, 0, 0)),
        ],
        out_specs=pl.BlockSpec((tm, d), lambda m: (m, 0)),
        out_shape=jax.ShapeDtypeStruct((rows, d), F32),
        scratch_shapes=[pltpu.VMEM((tm + 2 * SUBLANES, POOL_WIDTH), F32),
                        pltpu.VMEM((tm, POOL_WIDTH + SGU_WIDTH), BF16)],
        compiler_params=_cparams(1),
        name="out_proj",
    )(attn, pv, pv, pv, u, v, x2, mod4, pool_w, pool_scale, sgu_w, sgu_bt, w_out)


def _ffn_kernel(x_ref, xp_ref, xn_ref, mod_ref, g2_ref, wg_ref, wv_ref, cwg_ref, cwv_ref, cbg_ref,
                cbv_ref, wd_ref, fg_ref, o_ref, hs_ref, ag_ref, av_ref, act_ref, *, tm, seq, final):
    m = pl.program_id(0)
    f = pl.program_id(1)
    rows = BF16_SUBLANES
    grp = 128

    @pl.when(f == 0)
    def _():
        shift = mod_ref[3:4, :]
        gain = g2_ref[...] * (1.0 + mod_ref[4:5, :])
        _norm_rows(hs_ref, x_ref, tm, gain, shift, dst_offset=HALO)

        def halo(ref):
            xx = ref[...]
            inv = lax.rsqrt(jnp.mean(xx * xx, axis=-1, keepdims=True) + EPS)
            return xx * inv * gain + shift

        pos0 = (m * tm) % seq
        hn = jnp.where((pos0 + tm) % seq == 0, 0.0, halo(xn_ref))
        hp = jnp.where(pos0 == 0, 0.0, halo(xp_ref))
        pad = jnp.zeros_like(hp)
        hs_ref[0:HALO, :] = jnp.concatenate([pad, hp], axis=0).astype(BF16)
        hs_ref[HALO + tm:, :] = jnp.concatenate([hn, pad], axis=0).astype(BF16)
        o_ref[...] = jnp.zeros(o_ref.shape, F32)

    hs = hs_ref[...]
    part = FF_TILE // FF_PARTS
    parts = [slice(p * part, (p + 1) * part) for p in range(FF_PARTS)]
    for cs in parts:
        ag_ref[:, cs] = jnp.dot(hs, wg_ref[:, cs], preferred_element_type=F32)
        av_ref[:, cs] = jnp.dot(hs, wv_ref[:, cs], preferred_element_type=F32)

    row_in_grp = lax.broadcasted_iota(jnp.int32, (grp, 1), 0)

    def conv(a_ref, cw_ref, cb_ref, r, cs):
        prev = a_ref[HALO - 1 + r:HALO - 1 + r + grp, cs]
        cur = a_ref[HALO + r:HALO + r + grp, cs]
        nxt = a_ref[HALO + 1 + r:HALO + 1 + r + grp, cs]
        if r > 0 and r % seq == 0:
            prev = jnp.where(row_in_grp == 0, 0.0, prev)
        if r + grp < tm and (r + grp) % seq == 0:
            nxt = jnp.where(row_in_grp == grp - 1, 0.0, nxt)
        return cb_ref[:, cs] + prev * cw_ref[0:1, cs] + cur * cw_ref[1:2, cs] + nxt * cw_ref[2:3, cs]

    for cs in parts:
        for r in range(0, tm, grp):
            act = _silu(conv(ag_ref, cwg_ref, cbg_ref, r, cs)) * conv(av_ref, cwv_ref, cbv_ref, r, cs)
            act_ref[r:r + grp, cs] = act.astype(BF16)
    for r in range(0, tm, FF_DOWN_ROWS):
        rs = slice(r, r + FF_DOWN_ROWS)
        o_ref[rs, :] += jnp.dot(act_ref[rs, :], wd_ref[...], preferred_element_type=F32)

    @pl.when(f == pl.num_programs(1) - 1)
    def _():
        gate2 = mod_ref[5:6, :]
        fg = fg_ref[...]

        for r in range(0, tm, rows):
            y = x_ref[r:r + rows, :] + gate2 * o_ref[r:r + rows, :]
            if final:
                y = _rms(y, fg)
            o_ref[r:r + rows, :] = y


def _ffn(x2, mod4, layer, mod_row, g2, w_up, conv_w, conv_b, w_down, final_g, *, seq, tm, final):
    rows, d = x2.shape
    assert rows % tm == 0 and (seq % tm == 0 or tm % seq == 0) and seq % LANES == 0
    assert mod_row is not None or seq % tm == 0
    per_seq = max(seq // tm, 1)
    hb = tm // SUBLANES
    last_hb = rows // SUBLANES - 1
    nf = D_FF // FF_TILE

    def mrow(m):
        return mod_row if mod_row is not None else m // per_seq

    return pl.pallas_call(
        functools.partial(_ffn_kernel, tm=tm, seq=seq, final=final),
        grid=(rows // tm, nf),
        in_specs=[
            pl.BlockSpec((tm, d), lambda m, f: (m, 0), pipeline_mode=pl.Buffered(1)),
            pl.BlockSpec((SUBLANES, d), lambda m, f: (jnp.maximum(m * hb - 1, 0), 0)),
            pl.BlockSpec((SUBLANES, d), lambda m, f: (jnp.minimum((m + 1) * hb, last_hb), 0)),
            pl.BlockSpec((None, None, N_MOD, d), lambda m, f: (layer, mrow(m), 0, 0)),
            pl.BlockSpec((None, 1, d), lambda m, f: (layer, 0, 0)),
            pl.BlockSpec((None, None, d, FF_TILE), lambda m, f: (layer, f, 0, 0)),
            pl.BlockSpec((None, None, d, FF_TILE), lambda m, f: (layer, f + nf, 0, 0)),
            pl.BlockSpec((None, 3, FF_TILE), lambda m, f: (layer, 0, f)),
            pl.BlockSpec((None, 3, FF_TILE), lambda m, f: (layer, 0, f + nf)),
            pl.BlockSpec((None, 1, FF_TILE), lambda m, f: (layer, 0, f)),
            pl.BlockSpec((None, 1, FF_TILE), lambda m, f: (layer, 0, f + nf)),
            pl.BlockSpec((None, FF_TILE, d), lambda m, f: (layer, f, 0)),
            pl.BlockSpec((1, d), lambda m, f: (0, 0)),
        ],
        out_specs=pl.BlockSpec((tm, d), lambda m, f: (m, 0)),
        out_shape=jax.ShapeDtypeStruct((rows, d), F32),
        scratch_shapes=[pltpu.VMEM((tm + 2 * HALO, d), BF16),
                        pltpu.VMEM((tm + 2 * HALO, FF_TILE), F32),
                        pltpu.VMEM((tm + 2 * HALO, FF_TILE), F32),
                        pltpu.VMEM((tm, FF_TILE), BF16)],
        compiler_params=_cparams(2),
        name="ffn",
    )(x2, x2, x2, mod4, g2, w_up, w_up, conv_w, conv_w, conv_b, conv_b, w_down, final_g)


def _head_lane_order(a, n_heads):
    lead = a.shape[:-1]
    a = a.reshape(lead + (n_heads, 2, 2, ROT_PAIR))
    return jnp.swapaxes(a, -2, -3).reshape(lead + (n_heads * HEAD_DIM,))


def _rope_tables(seq):
    rows = seq // GRID_W
    row_ids = jnp.repeat(jnp.arange(rows), GRID_W).astype(F32)
    col_ids = jnp.tile(jnp.arange(GRID_W), rows).astype(F32)
    inv = 1.0 / (ROPE_BASE ** (jnp.arange(0, ROT_AXIS_DIM, 2, dtype=F32) / ROT_AXIS_DIM))
    ang_r = row_ids[:, None] * inv
    ang_c = col_ids[:, None] * inv
    cr, sr, cc, sc = jnp.cos(ang_r), jnp.sin(ang_r), jnp.cos(ang_c), jnp.sin(ang_c)
    cos = jnp.concatenate([cr, cc, cr, cc], axis=-1)
    sin_signed = jnp.concatenate([-sr, -sc, sr, sc], axis=-1)
    return cos, sin_signed


def kernel(x, c, ctx, c_ctx, norm1_g, norm2_g, w_ada, b_ada, w_in, q_norm_g, k_norm_g, attn_sink,
           pool_w, pool_scale, sgu_norm_g, sgu_w, sgu_b, w_out, w_up, conv_w, conv_b, w_down,
           final_norm_g):
    batch, seq, d = x.shape
    ctx_len = ctx.shape[1]
    depth = w_ada.shape[0]
    assert d == D_MODEL and batch + 1 <= MOD_ROWS
    ctx_row = batch

    cc = jnp.zeros((MOD_ROWS, d), F32).at[:batch].set(c).at[ctx_row].set(c_ctx)
    mod4 = _modulation(cc, w_ada, b_ada).reshape(depth, MOD_ROWS, N_MOD, d)
    rope_tabs = _rope_tables(seq)

    qk = ATTN_WIDTH + KV_WIDTH
    w_in_b = jnp.concatenate(
        [_head_lane_order(w_in[..., :qk], N_Q_HEADS + N_KV_HEADS), w_in[..., qk:]], axis=-1
    ).astype(BF16)
    w_out_b, w_down_b = w_out.astype(BF16), w_down.astype(BF16)
    w_up_b = jnp.swapaxes(w_up.reshape(depth, d, -1, FF_TILE), 1, 2).astype(BF16)
    pool_w_b, sgu_w_b = pool_w.astype(BF16), sgu_w.astype(BF16)
    g1, g2 = norm1_g[:, None], norm2_g[:, None]
    qg = _head_lane_order(q_norm_g, 1)[:, None]
    kg = _head_lane_order(k_norm_g, 1)[:, None]
    sg = sgu_norm_g[:, None]
    mixer = (pool_w_b, pool_scale[:, None], sgu_w_b, jnp.swapaxes(sgu_b, 1, 2), w_out_b)
    mlp = (g2, w_up_b, conv_w, conv_b[:, None], w_down_b, final_norm_g[None])

    x2 = x.reshape(batch * seq, d)
    xc2 = ctx.reshape(batch * ctx_len, d)
    lat_tm = 512
    ctx_tm = ctx_len
    ffn_tm = 1024
    attn_qblocks = 2
    for l in range(depth):
        last = l == depth - 1
        q, kv, pv, u, v = _in_proj(x2, mod4, l, None, g1, w_in_b, qg, kg, sg, rope_tabs,
                                   seq=seq, tm=lat_tm, tiles=range(6))
        if last:
            (kvc,) = _in_proj(xc2, mod4, l, ctx_row, g1, w_in_b, qg, kg, sg, None,
                              seq=ctx_len, tm=ctx_tm, tiles=(2,))
        else:
            qc, kvc, pvc, uc, vc = _in_proj(xc2, mod4, l, ctx_row, g1, w_in_b, qg, kg, sg, None,
                                            seq=ctx_len, tm=ctx_tm, tiles=range(6))
        attn = _attention(attn_sink[l], q, kv, kvc, batch=batch, seq=seq, ctx_len=ctx_len, band=True,
                          qblocks=attn_qblocks)
        x2 = _out_proj(attn, pv, u, v, x2, mod4, l, None, *mixer, seq=seq, tm=lat_tm)
        if not last:
            attn_c = _attention(attn_sink[l], qc, None, kvc, batch=batch, seq=ctx_len,
                                ctx_len=ctx_len, band=False, qblocks=attn_qblocks)
            xc2 = _out_proj(attn_c, pvc, uc, vc, xc2, mod4, l, ctx_row, *mixer, seq=ctx_len, tm=ctx_tm)
        x2 = _ffn(x2, mod4, l, None, *mlp, seq=seq, tm=ffn_tm, final=last)
        if not last:
            xc2 = _ffn(xc2, mod4, l, ctx_row, *mlp, seq=ctx_len, tm=ffn_tm, final=False)
    return x2.reshape(batch, seq, d)
```

```python
import functools

import jax
import jax.numpy as jnp
from jax import lax
from jax.experimental import pallas as pl
from jax.experimental.pallas import tpu as pltpu

F32 = jnp.float32
BF16 = jnp.bfloat16

D_MODEL = 2048
GRID_W = 64
HEAD_DIM = 128
N_Q_HEADS = 8
N_KV_HEADS = 2
GQA_GROUP = N_Q_HEADS // N_KV_HEADS
ATTN_WIDTH = N_Q_HEADS * HEAD_DIM
KV_WIDTH = N_KV_HEADS * HEAD_DIM
WINDOW = 128
BLOCK = 128
ROPE_BASE = 10000.0
ROT_AXIS_DIM = HEAD_DIM // 2
ROT_PAIR = ROT_AXIS_DIM // 2
POOL_WINDOWS = (2, 4, 8, 16)
POOL_WIDTH = 512
POOL_GROUP_DIM = 128
SGU_WIDTH = 512
N_SGU_HEADS = 4
SGU_CHUNK = 128
IN_WIDTH = 3072
D_FF = 5632
N_MOD = 6
EPS = 1e-6

SUBLANES = 8
BF16_SUBLANES = 16
LANES = 128
HALO = BF16_SUBLANES
MOD_ROWS = 16
IN_TILE = 512
FF_TILE = 512
FF_PARTS = 2
FF_DOWN_ROWS = 256
VMEM_LIMIT = 60 * 1024 * 1024
VMEM_SPILL_RESERVE = 8 * 1024 * 1024
CAST_BLOCK_BYTES = 6 * 1024 * 1024


def _cparams(n_axes):
    return pltpu.CompilerParams(
        dimension_semantics=("arbitrary",) * n_axes, vmem_limit_bytes=VMEM_LIMIT)


def _rms(x, g):
    return x * lax.rsqrt(jnp.mean(x * x, axis=-1, keepdims=True) + EPS) * g


def _gelu(x):
    return 0.5 * x * (1.0 + lax.erf(x * 0.7071067811865476))


def _silu(x):
    return x * jax.nn.sigmoid(x)


def _norm_rows(dst_ref, x_ref, n_rows, gain, shift, *, dst_offset=0):
    rows = BF16_SUBLANES
    for r in range(0, n_rows, rows):
        xx = x_ref[r:r + rows, :]
        inv = lax.rsqrt(jnp.mean(xx * xx, axis=-1, keepdims=True) + EPS)
        dst_ref[dst_offset + r:dst_offset + r + rows, :] = (xx * inv * gain + shift).astype(BF16)


def _resident(shape, index_map):
    return pl.BlockSpec(shape, index_map, pipeline_mode=pl.Buffered(1))


def _mod_kernel(c_ref, w_ref, b_ref, o_ref):
    c = c_ref[...]
    s = _silu(c).astype(BF16)
    o_ref[...] = jnp.dot(s, w_ref[...].astype(BF16), preferred_element_type=F32) + b_ref[...]


def _modulation(cc, w_ada, b_ada):
    depth, d, n = w_ada.shape
    tn = 1024
    return pl.pallas_call(
        _mod_kernel,
        grid=(depth, n // tn),
        in_specs=[
            pl.BlockSpec((MOD_ROWS, d), lambda l, j: (0, 0)),
            pl.BlockSpec((None, d, tn), lambda l, j: (l, 0, j)),
            pl.BlockSpec((None, 1, tn), lambda l, j: (l, 0, j)),
        ],
        out_specs=pl.BlockSpec((None, MOD_ROWS, tn), lambda l, j: (l, 0, j)),
        out_shape=jax.ShapeDtypeStruct((depth, MOD_ROWS, n), F32),
        compiler_params=_cparams(2),
        name="modulation",
    )(cc, w_ada, b_ada.reshape(depth, 1, n))


def _cast_kernel(w_ref, o_ref, *, head_order_cols):
    w = w_ref[...]
    if head_order_cols:
        wh = w[:, :head_order_cols]
        quarter = (lax.broadcasted_iota(jnp.int32, wh.shape, 1) % HEAD_DIM) // ROT_PAIR
        wh = jnp.where(quarter == 1, pltpu.roll(wh, head_order_cols - ROT_PAIR, 1),
                       jnp.where(quarter == 2, pltpu.roll(wh, ROT_PAIR, 1), wh))
        w = jnp.concatenate([wh, w[:, head_order_cols:]], axis=1)
    o_ref[...] = w.astype(BF16)


def _to_bf16(w, head_order_cols=0):
    depth, rows, cols = w.shape
    total = depth * rows
    block_rows = CAST_BLOCK_BYTES // (4 * cols)
    block_rows = 1 << (block_rows.bit_length() - 1)
    assert block_rows % BF16_SUBLANES == 0 and total % block_rows == 0
    out = pl.pallas_call(
        functools.partial(_cast_kernel, head_order_cols=head_order_cols),
        grid=(total // block_rows,),
        in_specs=[pl.BlockSpec((block_rows, cols), lambda i: (i, 0))],
        out_specs=pl.BlockSpec((block_rows, cols), lambda i: (i, 0)),
        out_shape=jax.ShapeDtypeStruct((total, cols), BF16),
        compiler_params=_cparams(1),
        name="to_bf16",
    )(w.reshape(total, cols))
    return out.reshape(depth, rows, cols)


def _in_kernel(*refs, tm, tiles, rope):
    n_in = 9 if rope else 7
    x_ref, mod_ref, g1_ref, w_ref, qg_ref, kg_ref, sg_ref = refs[:7]
    cos_ref, sin_ref = refs[7:9] if rope else (None, None)
    out_refs = refs[n_in:-1]
    hs_ref = refs[-1]
    names = [nm for nm, ids in (("q", (0, 1)), ("kv", (2,)), ("pv", (3,)), ("u", (4,)), ("v", (5,)))
             if any(t in tiles for t in ids)]
    outs = dict(zip(names, out_refs))
    grp = 128

    gain = g1_ref[...] * (1.0 + mod_ref[1:2, :])
    _norm_rows(hs_ref, x_ref, tm, gain, mod_ref[0:1, :])
    hs = hs_ref[...]

    def project(i):
        return jnp.dot(hs, w_ref[:, i * IN_TILE:(i + 1) * IN_TILE], preferred_element_type=F32)

    def head(acc, r, h, g):
        xh = _rms(acc[r:r + grp, h * HEAD_DIM:(h + 1) * HEAD_DIM], g)
        if rope:
            xh = xh * cos_ref[r:r + grp, :] + pltpu.roll(xh, ROT_AXIS_DIM, 1) * sin_ref[r:r + grp, :]
        return xh.astype(BF16)

    def epilogue(t, acc):
        if t in (0, 1):
            qg = qg_ref[...] * (HEAD_DIM ** -0.5)
            for r in range(0, tm, grp):
                for h in range(IN_TILE // HEAD_DIM):
                    c0 = t * IN_TILE + h * HEAD_DIM
                    outs["q"][r:r + grp, c0:c0 + HEAD_DIM] = head(acc, r, h, qg)
        elif t == 2:
            kg = kg_ref[...]
            for r in range(0, tm, grp):
                for h in range(N_KV_HEADS):
                    outs["kv"][r:r + grp, h * HEAD_DIM:(h + 1) * HEAD_DIM] = head(acc, r, h, kg)
                outs["kv"][r:r + grp, KV_WIDTH:] = acc[r:r + grp, KV_WIDTH:].astype(BF16)
        elif t == 3:
            outs["pv"][...] = acc
        elif t == 4:
            for r in range(0, tm, grp):
                outs["u"][r:r + grp, :] = _gelu(acc[r:r + grp, :])
        else:
            sg = sg_ref[...]
            for r in range(0, tm, grp):
                outs["v"][r:r + grp, :] = _rms(_gelu(acc[r:r + grp, :]), sg).astype(BF16)

    nxt = project(0)
    for i, t in enumerate(tiles):
        acc = nxt
        if i + 1 < len(tiles):
            nxt = project(i + 1)
        epilogue(t, acc)


def _in_proj(x2, mod4, layer, mod_row, g1, w_in, qg, kg, sg, rope_tabs, *, seq, tm, tiles):
    rows, d = x2.shape
    assert rows % tm == 0 and seq % tm == 0
    tiles = tuple(tiles)
    assert tiles == tuple(range(tiles[0], tiles[0] + len(tiles))) and tiles[0] % len(tiles) == 0
    width = len(tiles) * IN_TILE
    wblk = tiles[0] // len(tiles)
    rope = rope_tabs is not None
    per_seq = seq // tm

    def mrow(m):
        return mod_row if mod_row is not None else m // per_seq

    in_specs = [
        pl.BlockSpec((tm, d), lambda m: (m, 0)),
        pl.BlockSpec((None, None, N_MOD, d), lambda m: (layer, mrow(m), 0, 0)),
        pl.BlockSpec((None, 1, d), lambda m: (layer, 0, 0)),
        _resident((None, d, width), lambda m: (layer, 0, wblk)),
        pl.BlockSpec((None, 1, HEAD_DIM), lambda m: (layer, 0, 0)),
        pl.BlockSpec((None, 1, HEAD_DIM), lambda m: (layer, 0, 0)),
        pl.BlockSpec((None, 1, SGU_WIDTH), lambda m: (layer, 0, 0)),
    ]
    args = [x2, mod4, g1, w_in, qg, kg, sg]
    if rope:
        in_specs += [pl.BlockSpec((tm, HEAD_DIM), lambda m: (m % per_seq, 0)) for _ in range(2)]
        args += list(rope_tabs)
    out_specs, out_shape = [], []

    def add(width, dtype):
        out_specs.append(pl.BlockSpec((tm, width), lambda m: (m, 0)))
        out_shape.append(jax.ShapeDtypeStruct((rows, width), dtype))

    if 0 in tiles or 1 in tiles:
        assert 0 in tiles and 1 in tiles
        add(ATTN_WIDTH, BF16)
    if 2 in tiles:
        add(IN_TILE, BF16)
    if 3 in tiles:
        add(POOL_WIDTH, F32)
    if 4 in tiles:
        add(SGU_WIDTH, F32)
    if 5 in tiles:
        add(SGU_WIDTH, BF16)
    return pl.pallas_call(
        functools.partial(_in_kernel, tm=tm, tiles=tiles, rope=rope),
        grid=(rows // tm,),
        in_specs=in_specs,
        out_specs=out_specs,
        out_shape=out_shape,
        scratch_shapes=[pltpu.VMEM((tm, d), BF16)],
        compiler_params=_cparams(1),
        name="in_proj",
    )(*args)


def _attn_unit(sink_ref, q_ref, kv_ref, kvc_ref, o_ref, qrows, kh, start, bias):
    rows = GQA_GROUP * BLOCK
    nband = 3 * BLOCK
    nt = (((1,), (1,)), ((), ()))
    q0 = kh * GQA_GROUP * HEAD_DIM
    kcol = slice(kh * HEAD_DIM, (kh + 1) * HEAD_DIM)
    vcol = slice(KV_WIDTH + kh * HEAD_DIM, KV_WIDTH + (kh + 1) * HEAD_DIM)
    qs = jnp.concatenate(
        [q_ref[qrows, q0 + g * HEAD_DIM:q0 + (g + 1) * HEAD_DIM] for g in range(GQA_GROUP)], axis=0)
    s_ctx = lax.dot_general(qs, kvc_ref[:, kcol], nt, preferred_element_type=F32)
    row = lax.broadcasted_iota(jnp.int32, (rows, 1), 0)
    sink = jnp.zeros((rows, 1), F32)
    for g in range(GQA_GROUP):
        sink = jnp.where(row // BLOCK == g, sink_ref[kh * GQA_GROUP + g], sink)
    m = jnp.maximum(jnp.max(s_ctx, axis=-1, keepdims=True), sink)
    if kv_ref is not None:
        s_band = lax.dot_general(qs, kv_ref[pl.ds(start, nband), kcol], nt,
                                 preferred_element_type=F32) + bias
        m = jnp.maximum(m, jnp.max(s_band, axis=-1, keepdims=True))
    p_ctx = jnp.exp(s_ctx - m)
    den = jnp.sum(p_ctx, axis=-1, keepdims=True) + jnp.exp(sink - m)
    o = jnp.dot(p_ctx.astype(BF16), kvc_ref[:, vcol], preferred_element_type=F32)
    if kv_ref is not None:
        p_band = jnp.exp(s_band - m)
        den = den + jnp.sum(p_band, axis=-1, keepdims=True)
        o = o + jnp.dot(p_band.astype(BF16), kv_ref[pl.ds(start, nband), vcol],
                        preferred_element_type=F32)
    o = o / den
    for g in range(GQA_GROUP):
        o_ref[qrows, q0 + g * HEAD_DIM:q0 + (g + 1) * HEAD_DIM] = (
            o[g * BLOCK:(g + 1) * BLOCK, :].astype(BF16))


def _attn_kernel(*refs, seq, band, qblocks):
    if band:
        sink_ref, q_ref, kv_ref, kvc_ref, o_ref = refs
    else:
        sink_ref, q_ref, kvc_ref, o_ref = refs
        kv_ref = None
    for j in range(qblocks):
        n = pl.program_id(1) * qblocks + j
        qrows = slice(j * BLOCK, (j + 1) * BLOCK)
        start = bias = None
        if band:
            nband = 3 * BLOCK
            start = pl.multiple_of(jnp.clip((n - 1) * BLOCK, 0, seq - nband), BLOCK)
            dist = (n * BLOCK - start) + (lax.broadcasted_iota(jnp.int32, (BLOCK, nband), 0)
                                          - lax.broadcasted_iota(jnp.int32, (BLOCK, nband), 1))
            bias = jnp.where(jnp.abs(dist) <= WINDOW, 0.0, -jnp.inf).astype(F32)
            bias = jnp.concatenate([bias] * GQA_GROUP, axis=0)
        for kh in range(N_KV_HEADS):
            _attn_unit(sink_ref, q_ref, kv_ref, kvc_ref, o_ref, qrows, kh, start, bias)


def _attention(sink, q, kv, kvc, *, batch, seq, ctx_len, band, qblocks):
    tq = qblocks * BLOCK
    assert seq % tq == 0
    steps = seq // tq
    in_specs = [
        pl.BlockSpec(memory_space=pltpu.SMEM),
        pl.BlockSpec((tq, ATTN_WIDTH), lambda b, n: (b * steps + n, 0)),
    ]
    args = [sink, q]
    if band:
        in_specs.append(pl.BlockSpec((seq, 2 * KV_WIDTH), lambda b, n: (b, 0)))
        args.append(kv)
    in_specs.append(pl.BlockSpec((ctx_len, 2 * KV_WIDTH), lambda b, n: (b, 0)))
    args.append(kvc)
    return pl.pallas_call(
        functools.partial(_attn_kernel, seq=seq, band=band, qblocks=qblocks),
        grid=(batch, steps),
        in_specs=in_specs,
        out_specs=pl.BlockSpec((tq, ATTN_WIDTH), lambda b, n: (b * steps + n, 0)),
        out_shape=jax.ShapeDtypeStruct((batch * seq, ATTN_WIDTH), BF16),
        compiler_params=_cparams(2),
        name="attention",
    )(*args)


def _out_kernel(attn_ref, pv_ref, pvp_ref, pvn_ref, u_ref, v_ref, x_ref, mod_ref, pw_ref, ps_ref,
                sw_ref, sbt_ref, wo_ref, o_ref, pbuf_ref, mix_ref, *, tm, seq):
    m = pl.program_id(0)
    pos0 = (m * tm) % seq
    o_ref[...] = jnp.dot(attn_ref[...], wo_ref[0:ATTN_WIDTH, :], preferred_element_type=F32)
    pbuf_ref[0:SUBLANES, :] = jnp.where(pos0 == 0, 0.0, pvp_ref[...])
    pbuf_ref[SUBLANES:SUBLANES + tm, :] = pv_ref[...]
    pbuf_ref[SUBLANES + tm:, :] = jnp.where(pos0 + tm == seq, 0.0, pvn_ref[...])
    pos = pos0 + lax.broadcasted_iota(jnp.int32, (tm, 1), 0)
    ext = tm + 2 * SUBLANES

    def ahead(a, k):
        return pltpu.roll(a, ext - k, 0)

    for gi, w in enumerate(POOL_WINDOWS):
        c0 = gi * POOL_GROUP_DIM
        cols = slice(c0, c0 + POOL_GROUP_DIM)
        run, span = pbuf_ref[:, cols], 1
        while 2 * span < w:
            run = run + ahead(run, span)
            span *= 2
        first = run[0:tm] if w // 2 == SUBLANES else ahead(run, SUBLANES - w // 2)[0:tm]
        acc = first + run[SUBLANES:SUBLANES + tm]
        lo = jnp.maximum(pos - w // 2, 0)
        hi = jnp.minimum(pos - w // 2 + w, seq)
        y = (acc * (1.0 / (hi - lo).astype(F32)) - pv_ref[:, cols]).astype(BF16)
        z = jnp.dot(y, pw_ref[gi], preferred_element_type=F32) * ps_ref[:, cols]
        mix_ref[:, cols] = z.astype(BF16)
    n_chunks = tm // SGU_CHUNK
    for h in range(N_SGU_HEADS):
        cols = slice(h * LANES, (h + 1) * LANES)
        vh = jnp.concatenate(
            [v_ref[ci * SGU_CHUNK:(ci + 1) * SGU_CHUNK, cols] for ci in range(n_chunks)], axis=1)
        r = jnp.dot(sw_ref[h], vh, preferred_element_type=F32) + sbt_ref[:, h:h + 1]
        for ci in range(n_chunks):
            rs = slice(ci * SGU_CHUNK, (ci + 1) * SGU_CHUNK)
            gated = u_ref[rs, cols] * r[:, ci * SGU_CHUNK:(ci + 1) * SGU_CHUNK]
            mix_ref[rs, POOL_WIDTH + h * LANES:POOL_WIDTH + (h + 1) * LANES] = gated.astype(BF16)
    res = o_ref[...] + jnp.dot(mix_ref[...], wo_ref[ATTN_WIDTH:, :], preferred_element_type=F32)
    o_ref[...] = x_ref[...] + mod_ref[2:3, :] * res


def _out_proj(attn, pv, u, v, x2, mod4, layer, mod_row, pool_w, pool_scale, sgu_w, sgu_bt, w_out,
              *, seq, tm):
    rows, d = x2.shape
    assert rows % tm == 0 and seq % tm == 0 and tm % SGU_CHUNK == 0
    per_seq = seq // tm
    hb = tm // SUBLANES
    last_hb = rows // SUBLANES - 1

    def mrow(m):
        return mod_row if mod_row is not None else m // per_seq

    def layer_block(a):
        shape = a.shape[1:]
        return pl.BlockSpec((None,) + shape, lambda m: (layer,) + (0,) * len(shape))

    return pl.pallas_call(
        functools.partial(_out_kernel, tm=tm, seq=seq),
        grid=(rows // tm,),
        in_specs=[
            pl.BlockSpec((tm, ATTN_WIDTH), lambda m: (m, 0)),
            pl.BlockSpec((tm, POOL_WIDTH), lambda m: (m, 0)),
            pl.BlockSpec((SUBLANES, POOL_WIDTH), lambda m: (jnp.maximum(m * hb - 1, 0), 0)),
            pl.BlockSpec((SUBLANES, POOL_WIDTH), lambda m: (jnp.minimum((m + 1) * hb, last_hb), 0)),
            pl.BlockSpec((tm, SGU_WIDTH), lambda m: (m, 0)),
            pl.BlockSpec((tm, SGU_WIDTH), lambda m: (m, 0)),
            pl.BlockSpec((tm, d), lambda m: (m, 0)),
            pl.BlockSpec((None, None, N_MOD, d), lambda m: (layer, mrow(m), 0, 0)),
            layer_block(pool_w),
            layer_block(pool_scale),
            layer_block(sgu_w),
            layer_block(sgu_bt),
            _resident((None,) + w_out.shape[1:], lambda m: (layer, 0, 0)),
        ],
        out_specs=pl.BlockSpec((tm, d), lambda m: (m, 0)),
        out_shape=jax.ShapeDtypeStruct((rows, d), F32),
        scratch_shapes=[pltpu.VMEM((tm + 2 * SUBLANES, POOL_WIDTH), F32),
                        pltpu.VMEM((tm, POOL_WIDTH + SGU_WIDTH), BF16)],
        compiler_params=_cparams(1),
        name="out_proj",
    )(attn, pv, pv, pv, u, v, x2, mod4, pool_w, pool_scale, sgu_w, sgu_bt, w_out)


def _ffn_kernel(x_ref, xp_ref, xn_ref, mod_ref, g2_ref, wg_ref, wv_ref, cwg_ref, cwv_ref, cbg_ref,
                cbv_ref, wd_ref, fg_ref, o_ref, hs_ref, ag_ref, av_ref, act_ref, *, tm, seq, final):
    m = pl.program_id(0)
    f = pl.program_id(1)
    rows = BF16_SUBLANES
    grp = 128

    @pl.when(f == 0)
    def _():
        shift = mod_ref[3:4, :]
        gain = g2_ref[...] * (1.0 + mod_ref[4:5, :])
        _norm_rows(hs_ref, x_ref, tm, gain, shift, dst_offset=HALO)

        def halo(ref):
            xx = ref[...]
            inv = lax.rsqrt(jnp.mean(xx * xx, axis=-1, keepdims=True) + EPS)
            return xx * inv * gain + shift

        pos0 = (m * tm) % seq
        hn = jnp.where((pos0 + tm) % seq == 0, 0.0, halo(xn_ref))
        hp = jnp.where(pos0 == 0, 0.0, halo(xp_ref))
        pad = jnp.zeros_like(hp)
        hs_ref[0:HALO, :] = jnp.concatenate([pad, hp], axis=0).astype(BF16)
        hs_ref[HALO + tm:, :] = jnp.concatenate([hn, pad], axis=0).astype(BF16)
        o_ref[...] = jnp.zeros(o_ref.shape, F32)

    hs = hs_ref[...]
    part = FF_TILE // FF_PARTS
    parts = [slice(p * part, (p + 1) * part) for p in range(FF_PARTS)]
    for cs in parts:
        ag_ref[:, cs] = jnp.dot(hs, wg_ref[:, cs], preferred_element_type=F32)
        av_ref[:, cs] = jnp.dot(hs, wv_ref[:, cs], preferred_element_type=F32)

    row_in_grp = lax.broadcasted_iota(jnp.int32, (grp, 1), 0)

    def conv(a_ref, cw_ref, cb_ref, r, cs):
        prev = a_ref[HALO - 1 + r:HALO - 1 + r + grp, cs]
        cur = a_ref[HALO + r:HALO + r + grp, cs]
        nxt = a_ref[HALO + 1 + r:HALO + 1 + r + grp, cs]
        if r > 0 and r % seq == 0:
            prev = jnp.where(row_in_grp == 0, 0.0, prev)
        if r + grp < tm and (r + grp) % seq == 0:
            nxt = jnp.where(row_in_grp == grp - 1, 0.0, nxt)
        return cb_ref[:, cs] + prev * cw_ref[0:1, cs] + cur * cw_ref[1:2, cs] + nxt * cw_ref[2:3, cs]

    for cs in parts:
        for r in range(0, tm, grp):
            act = _silu(conv(ag_ref, cwg_ref, cbg_ref, r, cs)) * conv(av_ref, cwv_ref, cbv_ref, r, cs)
            act_ref[r:r + grp, cs] = act.astype(BF16)
    for r in range(0, tm, FF_DOWN_ROWS):
        rs = slice(r, r + FF_DOWN_ROWS)
        o_ref[rs, :] += jnp.dot(act_ref[rs, :], wd_ref[...], preferred_element_type=F32)

    @pl.when(f == pl.num_programs(1) - 1)
    def _():
        gate2 = mod_ref[5:6, :]
        fg = fg_ref[...]

        for r in range(0, tm, rows):
            y = x_ref[r:r + rows, :] + gate2 * o_ref[r:r + rows, :]
            if final:
                y = _rms(y, fg)
            o_ref[r:r + rows, :] = y


def _ffn(x2, mod4, layer, mod_row, g2, w_up, conv_w, conv_b, w_down, final_g, *, seq, tm, final):
    rows, d = x2.shape
    assert rows % tm == 0 and (seq % tm == 0 or tm % seq == 0) and seq % LANES == 0
    assert mod_row is not None or seq % tm == 0
    per_seq = max(seq // tm, 1)
    hb = tm // SUBLANES
    last_hb = rows // SUBLANES - 1
    nf = D_FF // FF_TILE

    def mrow(m):
        return mod_row if mod_row is not None else m // per_seq

    ext = tm + 2 * HALO
    working_set = (4 * tm * d * 4 + ext * d * 2 + 2 * ext * FF_TILE * 4 + tm * FF_TILE * 2
                   + 2 * 3 * d * FF_TILE * 2)
    x_buffers = 2 if working_set + VMEM_SPILL_RESERVE <= VMEM_LIMIT else 1
    return pl.pallas_call(
        functools.partial(_ffn_kernel, tm=tm, seq=seq, final=final),
        grid=(rows // tm, nf),
        in_specs=[
            pl.BlockSpec((tm, d), lambda m, f: (m, 0), pipeline_mode=pl.Buffered(x_buffers)),
            pl.BlockSpec((SUBLANES, d), lambda m, f: (jnp.maximum(m * hb - 1, 0), 0)),
            pl.BlockSpec((SUBLANES, d), lambda m, f: (jnp.minimum((m + 1) * hb, last_hb), 0)),
            pl.BlockSpec((None, None, N_MOD, d), lambda m, f: (layer, mrow(m), 0, 0)),
            pl.BlockSpec((None, 1, d), lambda m, f: (layer, 0, 0)),
            pl.BlockSpec((None, d, FF_TILE), lambda m, f: (layer, 0, f)),
            pl.BlockSpec((None, d, FF_TILE), lambda m, f: (layer, 0, f + nf)),
            pl.BlockSpec((None, 3, FF_TILE), lambda m, f: (layer, 0, f)),
            pl.BlockSpec((None, 3, FF_TILE), lambda m, f: (layer, 0, f + nf)),
            pl.BlockSpec((None, 1, FF_TILE), lambda m, f: (layer, 0, f)),
            pl.BlockSpec((None, 1, FF_TILE), lambda m, f: (layer, 0, f + nf)),
            pl.BlockSpec((None, FF_TILE, d), lambda m, f: (layer, f, 0)),
            pl.BlockSpec((1, d), lambda m, f: (0, 0)),
        ],
        out_specs=pl.BlockSpec((tm, d), lambda m, f: (m, 0)),
        out_shape=jax.ShapeDtypeStruct((rows, d), F32),
        scratch_shapes=[pltpu.VMEM((tm + 2 * HALO, d), BF16),
                        pltpu.VMEM((tm + 2 * HALO, FF_TILE), F32),
                        pltpu.VMEM((tm + 2 * HALO, FF_TILE), F32),
                        pltpu.VMEM((tm, FF_TILE), BF16)],
        compiler_params=_cparams(2),
        name="ffn",
    )(x2, x2, x2, mod4, g2, w_up, w_up, conv_w, conv_w, conv_b, conv_b, w_down, final_g)


def _head_lane_order(a, n_heads):
    lead = a.shape[:-1]
    a = a.reshape(lead + (n_heads, 2, 2, ROT_PAIR))
    return jnp.swapaxes(a, -2, -3).reshape(lead + (n_heads * HEAD_DIM,))


def _rope_tables(seq):
    rows = seq // GRID_W
    row_ids = jnp.repeat(jnp.arange(rows), GRID_W).astype(F32)
    col_ids = jnp.tile(jnp.arange(GRID_W), rows).astype(F32)
    inv = 1.0 / (ROPE_BASE ** (jnp.arange(0, ROT_AXIS_DIM, 2, dtype=F32) / ROT_AXIS_DIM))
    ang_r = row_ids[:, None] * inv
    ang_c = col_ids[:, None] * inv
    cr, sr, cc, sc = jnp.cos(ang_r), jnp.sin(ang_r), jnp.cos(ang_c), jnp.sin(ang_c)
    cos = jnp.concatenate([cr, cc, cr, cc], axis=-1)
    sin_signed = jnp.concatenate([-sr, -sc, sr, sc], axis=-1)
    return cos, sin_signed


def kernel(x, c, ctx, c_ctx, norm1_g, norm2_g, w_ada, b_ada, w_in, q_norm_g, k_norm_g, attn_sink,
           pool_w, pool_scale, sgu_norm_g, sgu_w, sgu_b, w_out, w_up, conv_w, conv_b, w_down,
           final_norm_g):
    batch, seq, d = x.shape
    ctx_len = ctx.shape[1]
    depth = w_ada.shape[0]
    assert d == D_MODEL and batch + 1 <= MOD_ROWS
    ctx_row = batch

    cc = jnp.zeros((MOD_ROWS, d), F32).at[:batch].set(c).at[ctx_row].set(c_ctx)
    mod4 = _modulation(cc, w_ada, b_ada).reshape(depth, MOD_ROWS, N_MOD, d)
    rope_tabs = _rope_tables(seq)

    w_in_b = _to_bf16(w_in, head_order_cols=ATTN_WIDTH + KV_WIDTH)
    w_out_b, w_up_b, w_down_b = (_to_bf16(w) for w in (w_out, w_up, w_down))
    pool_w_b, sgu_w_b = pool_w.astype(BF16), sgu_w.astype(BF16)
    g1, g2 = norm1_g[:, None], norm2_g[:, None]
    qg = _head_lane_order(q_norm_g, 1)[:, None]
    kg = _head_lane_order(k_norm_g, 1)[:, None]
    sg = sgu_norm_g[:, None]
    mixer = (pool_w_b, pool_scale[:, None], sgu_w_b, jnp.swapaxes(sgu_b, 1, 2), w_out_b)
    mlp = (g2, w_up_b, conv_w, conv_b[:, None], w_down_b, final_norm_g[None])

    x2 = x.reshape(batch * seq, d)
    xc2 = ctx.reshape(batch * ctx_len, d)
    lat_tm = 512
    ctx_tm = ctx_len
    ffn_tm = 512
    ctx_ffn_tm = 1024
    attn_qblocks = 4
    for l in range(depth):
        last = l == depth - 1
        q, kv, pv, u, v = _in_proj(x2, mod4, l, None, g1, w_in_b, qg, kg, sg, rope_tabs,
                                   seq=seq, tm=lat_tm, tiles=range(6))
        if last:
            (kvc,) = _in_proj(xc2, mod4, l, ctx_row, g1, w_in_b, qg, kg, sg, None,
                              seq=ctx_len, tm=ctx_tm, tiles=(2,))
        else:
            qc, kvc, pvc, uc, vc = _in_proj(xc2, mod4, l, ctx_row, g1, w_in_b, qg, kg, sg, None,
                                            seq=ctx_len, tm=ctx_tm, tiles=range(6))
        attn = _attention(attn_sink[l], q, kv, kvc, batch=batch, seq=seq, ctx_len=ctx_len, band=True,
                          qblocks=attn_qblocks)
        x2 = _out_proj(attn, pv, u, v, x2, mod4, l, None, *mixer, seq=seq, tm=lat_tm)
        if not last:
            attn_c = _attention(attn_sink[l], qc, None, kvc, batch=batch, seq=ctx_len,
                                ctx_len=ctx_len, band=False,
                                qblocks=min(attn_qblocks, ctx_len // BLOCK))
            xc2 = _out_proj(attn_c, pvc, uc, vc, xc2, mod4, l, ctx_row, *mixer, seq=ctx_len, tm=ctx_tm)
        x2 = _ffn(x2, mod4, l, None, *mlp, seq=seq, tm=ffn_tm, final=last)
        if not last:
            xc2 = _ffn(xc2, mod4, l, ctx_row, *mlp, seq=ctx_len, tm=ctx_ffn_tm, final=False)
    return x2.reshape(batch, seq, d)
```

```python
import functools

import jax
import jax.numpy as jnp
from jax import lax
from jax.experimental import pallas as pl
from jax.experimental.pallas import tpu as pltpu

F32 = jnp.float32
BF16 = jnp.bfloat16

D_MODEL = 2048
GRID_W = 64
HEAD_DIM = 128
N_Q_HEADS = 8
N_KV_HEADS = 2
GQA_GROUP = N_Q_HEADS // N_KV_HEADS
ATTN_WIDTH = N_Q_HEADS * HEAD_DIM
KV_WIDTH = N_KV_HEADS * HEAD_DIM
WINDOW = 128
BLOCK = 128
ROPE_BASE = 10000.0
ROT_AXIS_DIM = HEAD_DIM // 2
ROT_PAIR = ROT_AXIS_DIM // 2
POOL_WINDOWS = (2, 4, 8, 16)
POOL_WIDTH = 512
POOL_GROUP_DIM = 128
SGU_WIDTH = 512
N_SGU_HEADS = 4
SGU_CHUNK = 128
IN_WIDTH = 3072
D_FF = 5632
N_MOD = 6
EPS = 1e-6
LOG2_E = 1.4426950408889634

SUBLANES = 8
BF16_SUBLANES = 16
LANES = 128
HALO = BF16_SUBLANES
MOD_ROWS = 16
IN_TILE = 512
FF_TILE = 512
FF_PARTS = 2
FF_DOWN_ROWS = 512
VMEM_LIMIT = 60 * 1024 * 1024
VMEM_SPILL_RESERVE = 8 * 1024 * 1024
CAST_BLOCK_BYTES = 6 * 1024 * 1024


def _cparams(n_axes):
    return pltpu.CompilerParams(
        dimension_semantics=("arbitrary",) * n_axes, vmem_limit_bytes=VMEM_LIMIT)


def _rms(x, g):
    return x * lax.rsqrt(jnp.mean(x * x, axis=-1, keepdims=True) + EPS) * g


def _gelu(x):
    return 0.5 * x * (1.0 + lax.erf(x * 0.7071067811865476))


def _silu(x):
    return x * jax.nn.sigmoid(x)


def _norm_rows(dst_ref, x_ref, n_rows, gain, shift, *, dst_offset=0):
    rows = BF16_SUBLANES
    for r in range(0, n_rows, rows):
        xx = x_ref[r:r + rows, :]
        inv = lax.rsqrt(jnp.mean(xx * xx, axis=-1, keepdims=True) + EPS)
        dst_ref[dst_offset + r:dst_offset + r + rows, :] = (xx * inv * gain + shift).astype(BF16)


def _resident(shape, index_map):
    return pl.BlockSpec(shape, index_map, pipeline_mode=pl.Buffered(1))


def _mod_kernel(c_ref, w_ref, b_ref, o_ref):
    c = c_ref[...]
    s = _silu(c).astype(BF16)
    o_ref[...] = jnp.dot(s, w_ref[...].astype(BF16), preferred_element_type=F32) + b_ref[...]


def _modulation(cc, w_ada, b_ada):
    depth, d, n = w_ada.shape
    tn = 1024
    return pl.pallas_call(
        _mod_kernel,
        grid=(depth, n // tn),
        in_specs=[
            pl.BlockSpec((MOD_ROWS, d), lambda l, j: (0, 0)),
            pl.BlockSpec((None, d, tn), lambda l, j: (l, 0, j)),
            pl.BlockSpec((None, 1, tn), lambda l, j: (l, 0, j)),
        ],
        out_specs=pl.BlockSpec((None, MOD_ROWS, tn), lambda l, j: (l, 0, j)),
        out_shape=jax.ShapeDtypeStruct((depth, MOD_ROWS, n), F32),
        compiler_params=_cparams(2),
        name="modulation",
    )(cc, w_ada, b_ada.reshape(depth, 1, n))


def _cast_kernel(w_ref, o_ref, *, head_order_cols):
    w = w_ref[...]
    if head_order_cols:
        wh = w[:, :head_order_cols]
        quarter = (lax.broadcasted_iota(jnp.int32, wh.shape, 1) % HEAD_DIM) // ROT_PAIR
        wh = jnp.where(quarter == 1, pltpu.roll(wh, head_order_cols - ROT_PAIR, 1),
                       jnp.where(quarter == 2, pltpu.roll(wh, ROT_PAIR, 1), wh))
        w = jnp.concatenate([wh, w[:, head_order_cols:]], axis=1)
    o_ref[...] = w.astype(BF16)


def _to_bf16(w, head_order_cols=0):
    depth, rows, cols = w.shape
    total = depth * rows
    block_rows = CAST_BLOCK_BYTES // (4 * cols)
    block_rows = 1 << (block_rows.bit_length() - 1)
    assert block_rows % BF16_SUBLANES == 0 and total % block_rows == 0
    out = pl.pallas_call(
        functools.partial(_cast_kernel, head_order_cols=head_order_cols),
        grid=(total // block_rows,),
        in_specs=[pl.BlockSpec((block_rows, cols), lambda i: (i, 0))],
        out_specs=pl.BlockSpec((block_rows, cols), lambda i: (i, 0)),
        out_shape=jax.ShapeDtypeStruct((total, cols), BF16),
        compiler_params=_cparams(1),
        name="to_bf16",
    )(w.reshape(total, cols))
    return out.reshape(depth, rows, cols)


def _in_kernel(*refs, tm, tiles, rope):
    n_in = 9 if rope else 7
    x_ref, mod_ref, g1_ref, w_ref, qg_ref, kg_ref, sg_ref = refs[:7]
    cos_ref, sin_ref = refs[7:9] if rope else (None, None)
    out_refs = refs[n_in:-1]
    hs_ref = refs[-1]
    names = [nm for nm, ids in (("q", (0, 1)), ("kv", (2,)), ("pv", (3,)), ("u", (4,)), ("v", (5,)))
             if any(t in tiles for t in ids)]
    outs = dict(zip(names, out_refs))
    grp = 128

    gain = g1_ref[...] * (1.0 + mod_ref[1:2, :])
    _norm_rows(hs_ref, x_ref, tm, gain, mod_ref[0:1, :])
    hs = hs_ref[...]

    def project(i):
        return jnp.dot(hs, w_ref[:, i * IN_TILE:(i + 1) * IN_TILE], preferred_element_type=F32)

    def head(acc, r, h, g):
        xh = _rms(acc[r:r + grp, h * HEAD_DIM:(h + 1) * HEAD_DIM], g)
        if rope:
            xh = xh * cos_ref[r:r + grp, :] + pltpu.roll(xh, ROT_AXIS_DIM, 1) * sin_ref[r:r + grp, :]
        return xh.astype(BF16)

    def epilogue(t, acc):
        if t in (0, 1):
            qg = qg_ref[...] * (HEAD_DIM ** -0.5 * LOG2_E)
            for r in range(0, tm, grp):
                for h in range(IN_TILE // HEAD_DIM):
                    c0 = t * IN_TILE + h * HEAD_DIM
                    outs["q"][r:r + grp, c0:c0 + HEAD_DIM] = head(acc, r, h, qg)
        elif t == 2:
            kg = kg_ref[...]
            for r in range(0, tm, grp):
                for h in range(N_KV_HEADS):
                    outs["kv"][r:r + grp, h * HEAD_DIM:(h + 1) * HEAD_DIM] = head(acc, r, h, kg)
                outs["kv"][r:r + grp, KV_WIDTH:] = acc[r:r + grp, KV_WIDTH:].astype(BF16)
        elif t == 3:
            outs["pv"][...] = acc
        elif t == 4:
            for r in range(0, tm, grp):
                outs["u"][r:r + grp, :] = _gelu(acc[r:r + grp, :])
        else:
            sg = sg_ref[...]
            for r in range(0, tm, grp):
                outs["v"][r:r + grp, :] = _rms(_gelu(acc[r:r + grp, :]), sg).astype(BF16)

    nxt = project(0)
    for i, t in enumerate(tiles):
        acc = nxt
        if i + 1 < len(tiles):
            nxt = project(i + 1)
        epilogue(t, acc)


def _in_proj(x2, mod4, layer, mod_row, g1, w_in, qg, kg, sg, rope_tabs, *, seq, tm, tiles):
    rows, d = x2.shape
    assert rows % tm == 0 and seq % tm == 0
    tiles = tuple(tiles)
    assert tiles == tuple(range(tiles[0], tiles[0] + len(tiles))) and tiles[0] % len(tiles) == 0
    width = len(tiles) * IN_TILE
    wblk = tiles[0] // len(tiles)
    rope = rope_tabs is not None
    per_seq = seq // tm

    def mrow(m):
        return mod_row if mod_row is not None else m // per_seq

    in_specs = [
        pl.BlockSpec((tm, d), lambda m: (m, 0)),
        pl.BlockSpec((None, None, N_MOD, d), lambda m: (layer, mrow(m), 0, 0)),
        pl.BlockSpec((None, 1, d), lambda m: (layer, 0, 0)),
        _resident((None, d, width), lambda m: (layer, 0, wblk)),
        pl.BlockSpec((None, 1, HEAD_DIM), lambda m: (layer, 0, 0)),
        pl.BlockSpec((None, 1, HEAD_DIM), lambda m: (layer, 0, 0)),
        pl.BlockSpec((None, 1, SGU_WIDTH), lambda m: (layer, 0, 0)),
    ]
    args = [x2, mod4, g1, w_in, qg, kg, sg]
    if rope:
        in_specs += [pl.BlockSpec((tm, HEAD_DIM), lambda m: (m % per_seq, 0)) for _ in range(2)]
        args += list(rope_tabs)
    out_specs, out_shape = [], []

    def add(width, dtype):
        out_specs.append(pl.BlockSpec((tm, width), lambda m: (m, 0)))
        out_shape.append(jax.ShapeDtypeStruct((rows, width), dtype))

    if 0 in tiles or 1 in tiles:
        assert 0 in tiles and 1 in tiles
        add(ATTN_WIDTH, BF16)
    if 2 in tiles:
        add(IN_TILE, BF16)
    if 3 in tiles:
        add(POOL_WIDTH, F32)
    if 4 in tiles:
        add(SGU_WIDTH, F32)
    if 5 in tiles:
        add(SGU_WIDTH, BF16)
    return pl.pallas_call(
        functools.partial(_in_kernel, tm=tm, tiles=tiles, rope=rope),
        grid=(rows // tm,),
        in_specs=in_specs,
        out_specs=out_specs,
        out_shape=out_shape,
        scratch_shapes=[pltpu.VMEM((tm, d), BF16)],
        compiler_params=_cparams(1),
        name="in_proj",
    )(*args)


def _attn_unit(sink_ref, q_ref, kv_ref, kvc_ref, o_ref, qrows, kh, start, bias):
    rows = GQA_GROUP * BLOCK
    nband = 3 * BLOCK
    nt = (((1,), (1,)), ((), ()))
    q0 = kh * GQA_GROUP * HEAD_DIM
    kcol = slice(kh * HEAD_DIM, (kh + 1) * HEAD_DIM)
    vcol = slice(KV_WIDTH + kh * HEAD_DIM, KV_WIDTH + (kh + 1) * HEAD_DIM)
    qs = jnp.concatenate(
        [q_ref[qrows, q0 + g * HEAD_DIM:q0 + (g + 1) * HEAD_DIM] for g in range(GQA_GROUP)], axis=0)
    s_ctx = lax.dot_general(qs, kvc_ref[:, kcol], nt, preferred_element_type=F32)
    row = lax.broadcasted_iota(jnp.int32, (rows, 1), 0)
    sink = jnp.zeros((rows, 1), F32)
    for g in range(GQA_GROUP):
        sink = jnp.where(row // BLOCK == g, sink_ref[kh * GQA_GROUP + g] * LOG2_E, sink)
    m = jnp.maximum(jnp.max(s_ctx, axis=-1, keepdims=True), sink)
    if kv_ref is not None:
        s_band = lax.dot_general(qs, kv_ref[pl.ds(start, nband), kcol], nt,
                                 preferred_element_type=F32) + bias
        m = jnp.maximum(m, jnp.max(s_band, axis=-1, keepdims=True))
    p_ctx = jnp.exp2(s_ctx - m)
    den = jnp.sum(p_ctx, axis=-1, keepdims=True) + jnp.exp2(sink - m)
    o = jnp.dot(p_ctx.astype(BF16), kvc_ref[:, vcol], preferred_element_type=F32)
    if kv_ref is not None:
        p_band = jnp.exp2(s_band - m)
        den = den + jnp.sum(p_band, axis=-1, keepdims=True)
        o = o + jnp.dot(p_band.astype(BF16), kv_ref[pl.ds(start, nband), vcol],
                        preferred_element_type=F32)
    o = o * (1.0 / den)
    for g in range(GQA_GROUP):
        o_ref[qrows, q0 + g * HEAD_DIM:q0 + (g + 1) * HEAD_DIM] = (
            o[g * BLOCK:(g + 1) * BLOCK, :].astype(BF16))


def _attn_kernel(*refs, seq, band, qblocks):
    if band:
        sink_ref, q_ref, kv_ref, kvc_ref, o_ref = refs
    else:
        sink_ref, q_ref, kvc_ref, o_ref = refs
        kv_ref = None
    for j in range(qblocks):
        n = pl.program_id(1) * qblocks + j
        qrows = slice(j * BLOCK, (j + 1) * BLOCK)
        start = bias = None
        if band:
            nband = 3 * BLOCK
            start = pl.multiple_of(jnp.clip((n - 1) * BLOCK, 0, seq - nband), BLOCK)
            dist = (n * BLOCK - start) + (lax.broadcasted_iota(jnp.int32, (BLOCK, nband), 0)
                                          - lax.broadcasted_iota(jnp.int32, (BLOCK, nband), 1))
            bias = jnp.where(jnp.abs(dist) <= WINDOW, 0.0, -jnp.inf).astype(F32)
            bias = jnp.concatenate([bias] * GQA_GROUP, axis=0)
        for kh in range(N_KV_HEADS):
            _attn_unit(sink_ref, q_ref, kv_ref, kvc_ref, o_ref, qrows, kh, start, bias)


def _attention(sink, q, kv, kvc, *, batch, seq, ctx_len, band, qblocks):
    tq = qblocks * BLOCK
    assert seq % tq == 0
    steps = seq // tq
    in_specs = [
        pl.BlockSpec(memory_space=pltpu.SMEM),
        pl.BlockSpec((tq, ATTN_WIDTH), lambda b, n: (b * steps + n, 0)),
    ]
    args = [sink, q]
    if band:
        in_specs.append(pl.BlockSpec((seq, 2 * KV_WIDTH), lambda b, n: (b, 0)))
        args.append(kv)
    in_specs.append(pl.BlockSpec((ctx_len, 2 * KV_WIDTH), lambda b, n: (b, 0)))
    args.append(kvc)
    return pl.pallas_call(
        functools.partial(_attn_kernel, seq=seq, band=band, qblocks=qblocks),
        grid=(batch, steps),
        in_specs=in_specs,
        out_specs=pl.BlockSpec((tq, ATTN_WIDTH), lambda b, n: (b * steps + n, 0)),
        out_shape=jax.ShapeDtypeStruct((batch * seq, ATTN_WIDTH), BF16),
        compiler_params=_cparams(2),
        name="attention",
    )(*args)


def _out_kernel(attn_ref, pv_ref, pvp_ref, pvn_ref, u_ref, v_ref, x_ref, mod_ref, pw_ref, ps_ref,
                sw_ref, sbt_ref, wo_ref, o_ref, pbuf_ref, mix_ref, *, tm, seq):
    m = pl.program_id(0)
    pos0 = (m * tm) % seq
    o_ref[...] = jnp.dot(attn_ref[...], wo_ref[0:ATTN_WIDTH, :], preferred_element_type=F32)
    pbuf_ref[0:SUBLANES, :] = jnp.where(pos0 == 0, 0.0, pvp_ref[...])
    pbuf_ref[SUBLANES:SUBLANES + tm, :] = pv_ref[...]
    pbuf_ref[SUBLANES + tm:, :] = jnp.where(pos0 + tm == seq, 0.0, pvn_ref[...])
    pos = pos0 + lax.broadcasted_iota(jnp.int32, (tm, 1), 0)
    ext = tm + 2 * SUBLANES

    def ahead(a, k):
        return pltpu.roll(a, ext - k, 0)

    for gi, w in enumerate(POOL_WINDOWS):
        c0 = gi * POOL_GROUP_DIM
        cols = slice(c0, c0 + POOL_GROUP_DIM)
        run, span = pbuf_ref[:, cols], 1
        while 2 * span < w:
            run = run + ahead(run, span)
            span *= 2
        first = run[0:tm] if w // 2 == SUBLANES else ahead(run, SUBLANES - w // 2)[0:tm]
        acc = first + run[SUBLANES:SUBLANES + tm]
        lo = jnp.maximum(pos - w // 2, 0)
        hi = jnp.minimum(pos - w // 2 + w, seq)
        y = (acc * (1.0 / (hi - lo).astype(F32)) - pv_ref[:, cols]).astype(BF16)
        z = jnp.dot(y, pw_ref[gi], preferred_element_type=F32) * ps_ref[:, cols]
        mix_ref[:, cols] = z.astype(BF16)
    n_chunks = tm // SGU_CHUNK
    for h in range(N_SGU_HEADS):
        cols = slice(h * LANES, (h + 1) * LANES)
        vh = jnp.concatenate(
            [v_ref[ci * SGU_CHUNK:(ci + 1) * SGU_CHUNK, cols] for ci in range(n_chunks)], axis=1)
        r = jnp.dot(sw_ref[h], vh, preferred_element_type=F32) + sbt_ref[:, h:h + 1]
        for ci in range(n_chunks):
            rs = slice(ci * SGU_CHUNK, (ci + 1) * SGU_CHUNK)
            gated = u_ref[rs, cols] * r[:, ci * SGU_CHUNK:(ci + 1) * SGU_CHUNK]
            mix_ref[rs, POOL_WIDTH + h * LANES:POOL_WIDTH + (h + 1) * LANES] = gated.astype(BF16)
    res = o_ref[...] + jnp.dot(mix_ref[...], wo_ref[ATTN_WIDTH:, :], preferred_element_type=F32)
    o_ref[...] = x_ref[...] + mod_ref[2:3, :] * res


def _out_proj(attn, pv, u, v, x2, mod4, layer, mod_row, pool_w, pool_scale, sgu_w, sgu_bt, w_out,
              *, seq, tm):
    rows, d = x2.shape
    assert rows % tm == 0 and seq % tm == 0 and tm % SGU_CHUNK == 0
    per_seq = seq // tm
    hb = tm // SUBLANES
    last_hb = rows // SUBLANES - 1

    def mrow(m):
        return mod_row if mod_row is not None else m // per_seq

    def layer_block(a):
        shape = a.shape[1:]
        return pl.BlockSpec((None,) + shape, lambda m: (layer,) + (0,) * len(shape))

    return pl.pallas_call(
        functools.partial(_out_kernel, tm=tm, seq=seq),
        grid=(rows // tm,),
        in_specs=[
            pl.BlockSpec((tm, ATTN_WIDTH), lambda m: (m, 0)),
            pl.BlockSpec((tm, POOL_WIDTH), lambda m: (m, 0)),
            pl.BlockSpec((SUBLANES, POOL_WIDTH), lambda m: (jnp.maximum(m * hb - 1, 0), 0)),
            pl.BlockSpec((SUBLANES, POOL_WIDTH), lambda m: (jnp.minimum((m + 1) * hb, last_hb), 0)),
            pl.BlockSpec((tm, SGU_WIDTH), lambda m: (m, 0)),
            pl.BlockSpec((tm, SGU_WIDTH), lambda m: (m, 0)),
            pl.BlockSpec((tm, d), lambda m: (m, 0)),
            pl.BlockSpec((None, None, N_MOD, d), lambda m: (layer, mrow(m), 0, 0)),
            layer_block(pool_w),
            layer_block(pool_scale),
            layer_block(sgu_w),
            layer_block(sgu_bt),
            _resident((None,) + w_out.shape[1:], lambda m: (layer, 0, 0)),
        ],
        out_specs=pl.BlockSpec((tm, d), lambda m: (m, 0)),
        out_shape=jax.ShapeDtypeStruct((rows, d), F32),
        scratch_shapes=[pltpu.VMEM((tm + 2 * SUBLANES, POOL_WIDTH), F32),
                        pltpu.VMEM((tm, POOL_WIDTH + SGU_WIDTH), BF16)],
        compiler_params=_cparams(1),
        name="out_proj",
    )(attn, pv, pv, pv, u, v, x2, mod4, pool_w, pool_scale, sgu_w, sgu_bt, w_out)


def _ffn_kernel(x_ref, xp_ref, xn_ref, mod_ref, g2_ref, wg_ref, wv_ref, cwg_ref, cwv_ref, cbg_ref,
                cbv_ref, wd_ref, fg_ref, o_ref, hs_ref, ag_ref, av_ref, act_ref, *, tm, seq, final):
    m = pl.program_id(0)
    f = pl.program_id(1)
    rows = BF16_SUBLANES
    grp = 128

    @pl.when(f == 0)
    def _():
        shift = mod_ref[3:4, :]
        gain = g2_ref[...] * (1.0 + mod_ref[4:5, :])
        _norm_rows(hs_ref, x_ref, tm, gain, shift, dst_offset=HALO)

        def halo(ref):
            xx = ref[...]
            inv = lax.rsqrt(jnp.mean(xx * xx, axis=-1, keepdims=True) + EPS)
            return xx * inv * gain + shift

        pos0 = (m * tm) % seq
        hn = jnp.where((pos0 + tm) % seq == 0, 0.0, halo(xn_ref))
        hp = jnp.where(pos0 == 0, 0.0, halo(xp_ref))
        pad = jnp.zeros_like(hp)
        hs_ref[0:HALO, :] = jnp.concatenate([pad, hp], axis=0).astype(BF16)
        hs_ref[HALO + tm:, :] = jnp.concatenate([hn, pad], axis=0).astype(BF16)
        o_ref[...] = jnp.zeros(o_ref.shape, F32)

    hs = hs_ref[...]
    part = FF_TILE // FF_PARTS
    parts = [slice(p * part, (p + 1) * part) for p in range(FF_PARTS)]
    for cs in parts:
        ag_ref[:, cs] = jnp.dot(hs, wg_ref[:, cs], preferred_element_type=F32)
        av_ref[:, cs] = jnp.dot(hs, wv_ref[:, cs], preferred_element_type=F32)

    row_in_grp = lax.broadcasted_iota(jnp.int32, (grp, 1), 0)

    def conv(a_ref, cw_ref, cb_ref, r, cs):
        prev = a_ref[HALO - 1 + r:HALO - 1 + r + grp, cs]
        cur = a_ref[HALO + r:HALO + r + grp, cs]
        nxt = a_ref[HALO + 1 + r:HALO + 1 + r + grp, cs]
        if r > 0 and r % seq == 0:
            prev = jnp.where(row_in_grp == 0, 0.0, prev)
        if r + grp < tm and (r + grp) % seq == 0:
            nxt = jnp.where(row_in_grp == grp - 1, 0.0, nxt)
        return cb_ref[:, cs] + prev * cw_ref[0:1, cs] + cur * cw_ref[1:2, cs] + nxt * cw_ref[2:3, cs]

    for cs in parts:
        for r in range(0, tm, grp):
            act = _silu(conv(ag_ref, cwg_ref, cbg_ref, r, cs)) * conv(av_ref, cwv_ref, cbv_ref, r, cs)
            act_ref[r:r + grp, cs] = act.astype(BF16)
    for r in range(0, tm, FF_DOWN_ROWS):
        rs = slice(r, min(r + FF_DOWN_ROWS, tm))
        o_ref[rs, :] += jnp.dot(act_ref[rs, :], wd_ref[...], preferred_element_type=F32)

    @pl.when(f == pl.num_programs(1) - 1)
    def _():
        gate2 = mod_ref[5:6, :]
        fg = fg_ref[...]

        for r in range(0, tm, rows):
            y = x_ref[r:r + rows, :] + gate2 * o_ref[r:r + rows, :]
            if final:
                y = _rms(y, fg)
            o_ref[r:r + rows, :] = y


def _ffn(x2, mod4, layer, mod_row, g2, w_up, conv_w, conv_b, w_down, final_g, *, seq, tm, final):
    rows, d = x2.shape
    assert rows % tm == 0 and (seq % tm == 0 or tm % seq == 0) and seq % LANES == 0
    assert mod_row is not None or seq % tm == 0
    per_seq = max(seq // tm, 1)
    hb = tm // SUBLANES
    last_hb = rows // SUBLANES - 1
    nf = D_FF // FF_TILE

    def mrow(m):
        return mod_row if mod_row is not None else m // per_seq

    ext = tm + 2 * HALO
    working_set = (4 * tm * d * 4 + ext * d * 2 + 2 * ext * FF_TILE * 4 + tm * FF_TILE * 2
                   + 2 * 3 * d * FF_TILE * 2)
    x_buffers = 2 if working_set + VMEM_SPILL_RESERVE <= VMEM_LIMIT else 1
    return pl.pallas_call(
        functools.partial(_ffn_kernel, tm=tm, seq=seq, final=final),
        grid=(rows // tm, nf),
        in_specs=[
            pl.BlockSpec((tm, d), lambda m, f: (m, 0), pipeline_mode=pl.Buffered(x_buffers)),
            pl.BlockSpec((SUBLANES, d), lambda m, f: (jnp.maximum(m * hb - 1, 0), 0)),
            pl.BlockSpec((SUBLANES, d), lambda m, f: (jnp.minimum((m + 1) * hb, last_hb), 0)),
            pl.BlockSpec((None, None, N_MOD, d), lambda m, f: (layer, mrow(m), 0, 0)),
            pl.BlockSpec((None, 1, d), lambda m, f: (layer, 0, 0)),
            pl.BlockSpec((None, d, FF_TILE), lambda m, f: (layer, 0, f)),
            pl.BlockSpec((None, d, FF_TILE), lambda m, f: (layer, 0, f + nf)),
            pl.BlockSpec((None, 3, FF_TILE), lambda m, f: (layer, 0, f)),
            pl.BlockSpec((None, 3, FF_TILE), lambda m, f: (layer, 0, f + nf)),
            pl.BlockSpec((None, 1, FF_TILE), lambda m, f: (layer, 0, f)),
            pl.BlockSpec((None, 1, FF_TILE), lambda m, f: (layer, 0, f + nf)),
            pl.BlockSpec((None, FF_TILE, d), lambda m, f: (layer, f, 0)),
            pl.BlockSpec((1, d), lambda m, f: (0, 0)),
        ],
        out_specs=pl.BlockSpec((tm, d), lambda m, f: (m, 0)),
        out_shape=jax.ShapeDtypeStruct((rows, d), F32),
        scratch_shapes=[pltpu.VMEM((ext, d), BF16),
                        pltpu.VMEM((ext, FF_TILE), F32),
                        pltpu.VMEM((ext, FF_TILE), F32),
                        pltpu.VMEM((tm, FF_TILE), BF16)],
        compiler_params=_cparams(2),
        name="ffn",
    )(x2, x2, x2, mod4, g2, w_up, w_up, conv_w, conv_w, conv_b, conv_b, w_down, final_g)


def _head_lane_order(a, n_heads):
    lead = a.shape[:-1]
    a = a.reshape(lead + (n_heads, 2, 2, ROT_PAIR))
    return jnp.swapaxes(a, -2, -3).reshape(lead + (n_heads * HEAD_DIM,))


def _rope_tables(seq):
    rows = seq // GRID_W
    row_ids = jnp.repeat(jnp.arange(rows), GRID_W).astype(F32)
    col_ids = jnp.tile(jnp.arange(GRID_W), rows).astype(F32)
    inv = 1.0 / (ROPE_BASE ** (jnp.arange(0, ROT_AXIS_DIM, 2, dtype=F32) / ROT_AXIS_DIM))
    ang_r = row_ids[:, None] * inv
    ang_c = col_ids[:, None] * inv
    cr, sr, cc, sc = jnp.cos(ang_r), jnp.sin(ang_r), jnp.cos(ang_c), jnp.sin(ang_c)
    cos = jnp.concatenate([cr, cc, cr, cc], axis=-1)
    sin_signed = jnp.concatenate([-sr, -sc, sr, sc], axis=-1)
    return cos, sin_signed


def kernel(x, c, ctx, c_ctx, norm1_g, norm2_g, w_ada, b_ada, w_in, q_norm_g, k_norm_g, attn_sink,
           pool_w, pool_scale, sgu_norm_g, sgu_w, sgu_b, w_out, w_up, conv_w, conv_b, w_down,
           final_norm_g):
    batch, seq, d = x.shape
    ctx_len = ctx.shape[1]
    depth = w_ada.shape[0]
    assert d == D_MODEL and batch + 1 <= MOD_ROWS
    ctx_row = batch

    cc = jnp.zeros((MOD_ROWS, d), F32).at[:batch].set(c).at[ctx_row].set(c_ctx)
    mod4 = _modulation(cc, w_ada, b_ada).reshape(depth, MOD_ROWS, N_MOD, d)
    rope_tabs = _rope_tables(seq)

    w_in_b = _to_bf16(w_in, head_order_cols=ATTN_WIDTH + KV_WIDTH)
    w_out_b, w_up_b, w_down_b = (_to_bf16(w) for w in (w_out, w_up, w_down))
    pool_w_b, sgu_w_b = pool_w.astype(BF16), sgu_w.astype(BF16)
    g1, g2 = norm1_g[:, None], norm2_g[:, None]
    qg = _head_lane_order(q_norm_g, 1)[:, None]
    kg = _head_lane_order(k_norm_g, 1)[:, None]
    sg = sgu_norm_g[:, None]
    mixer = (pool_w_b, pool_scale[:, None], sgu_w_b, jnp.swapaxes(sgu_b, 1, 2), w_out_b)
    mlp = (g2, w_up_b, conv_w, conv_b[:, None], w_down_b, final_norm_g[None])

    x2 = x.reshape(batch * seq, d)
    xc2 = ctx.reshape(batch * ctx_len, d)
    lat_tm = 512
    in_tm = 512
    ctx_tm = ctx_len
    ffn_tm = 1024
    ctx_ffn_tm = 1024
    attn_qblocks = 4
    for l in range(depth):
        last = l == depth - 1
        q, kv, pv, u, v = _in_proj(x2, mod4, l, None, g1, w_in_b, qg, kg, sg, rope_tabs,
                                   seq=seq, tm=in_tm, tiles=range(6))
        if last:
            (kvc,) = _in_proj(xc2, mod4, l, ctx_row, g1, w_in_b, qg, kg, sg, None,
                              seq=ctx_len, tm=ctx_tm, tiles=(2,))
        else:
            qc, kvc, pvc, uc, vc = _in_proj(xc2, mod4, l, ctx_row, g1, w_in_b, qg, kg, sg, None,
                                            seq=ctx_len, tm=ctx_tm, tiles=range(6))
        attn = _attention(attn_sink[l], q, kv, kvc, batch=batch, seq=seq, ctx_len=ctx_len, band=True,
                          qblocks=attn_qblocks)
        x2 = _out_proj(attn, pv, u, v, x2, mod4, l, None, *mixer, seq=seq, tm=lat_tm)
        if not last:
            attn_c = _attention(attn_sink[l], qc, None, kvc, batch=batch, seq=ctx_len,
                                ctx_len=ctx_len, band=False,
                                qblocks=min(attn_qblocks, ctx_len // BLOCK))
            xc2 = _out_proj(attn_c, pvc, uc, vc, xc2, mod4, l, ctx_row, *mixer, seq=ctx_len, tm=ctx_tm)
        x2 = _ffn(x2, mod4, l, None, *mlp, seq=seq, tm=ffn_tm, final=last)
        if not last:
            xc2 = _ffn(xc2, mod4, l, ctx_row, *mlp, seq=ctx_len, tm=ctx_ffn_tm, final=False)
    return x2.reshape(batch, seq, d)
```

```python
import functools

import jax
import jax.numpy as jnp
from jax import lax
from jax.experimental import pallas as pl
from jax.experimental.pallas import tpu as pltpu

F32 = jnp.float32
BF16 = jnp.bfloat16

D_MODEL = 2048
GRID_W = 64
HEAD_DIM = 128
N_Q_HEADS = 8
N_KV_HEADS = 2
GQA_GROUP = N_Q_HEADS // N_KV_HEADS
ATTN_WIDTH = N_Q_HEADS * HEAD_DIM
KV_WIDTH = N_KV_HEADS * HEAD_DIM
WINDOW = 128
BLOCK = 128
ROPE_BASE = 10000.0
ROT_AXIS_DIM = HEAD_DIM // 2
ROT_PAIR = ROT_AXIS_DIM // 2
POOL_WINDOWS = (2, 4, 8, 16)
POOL_WIDTH = 512
POOL_GROUP_DIM = 128
SGU_WIDTH = 512
N_SGU_HEADS = 4
SGU_CHUNK = 128
IN_WIDTH = 3072
D_FF = 5632
N_MOD = 6
EPS = 1e-6
LOG2_E = 1.4426950408889634

SUBLANES = 8
BF16_SUBLANES = 16
LANES = 128
HALO = BF16_SUBLANES
MOD_ROWS = 16
IN_TILE = 512
FF_TILE = 512
FF_PARTS = 2
FF_DOWN_ROWS = 512
VMEM_LIMIT = 60 * 1024 * 1024
CAST_BLOCK_BYTES = 6 * 1024 * 1024


def _cparams(n_axes):
    return pltpu.CompilerParams(
        dimension_semantics=("arbitrary",) * n_axes, vmem_limit_bytes=VMEM_LIMIT)


def _rms(x, g):
    return x * lax.rsqrt(jnp.mean(x * x, axis=-1, keepdims=True) + EPS) * g


def _gelu(x):
    return 0.5 * x * (1.0 + lax.erf(x * 0.7071067811865476))


def _silu(x):
    return x * jax.nn.sigmoid(x)


def _norm_rows(dst_ref, x_ref, n_rows, gain, shift, *, dst_offset=0):
    rows = BF16_SUBLANES
    for r in range(0, n_rows, rows):
        xx = x_ref[r:r + rows, :]
        inv = lax.rsqrt(jnp.mean(xx * xx, axis=-1, keepdims=True) + EPS)
        dst_ref[dst_offset + r:dst_offset + r + rows, :] = (xx * inv * gain + shift).astype(BF16)


def _resident(shape, index_map):
    return pl.BlockSpec(shape, index_map, pipeline_mode=pl.Buffered(1))


def _mod_kernel(c_ref, w_ref, b_ref, o_ref):
    c = c_ref[...]
    s = _silu(c).astype(BF16)
    o_ref[...] = jnp.dot(s, w_ref[...].astype(BF16), preferred_element_type=F32) + b_ref[...]


def _modulation(cc, w_ada, b_ada):
    depth, d, n = w_ada.shape
    tn = 1024
    return pl.pallas_call(
        _mod_kernel,
        grid=(depth, n // tn),
        in_specs=[
            pl.BlockSpec((MOD_ROWS, d), lambda l, j: (0, 0)),
            pl.BlockSpec((None, d, tn), lambda l, j: (l, 0, j)),
            pl.BlockSpec((None, 1, tn), lambda l, j: (l, 0, j)),
        ],
        out_specs=pl.BlockSpec((None, MOD_ROWS, tn), lambda l, j: (l, 0, j)),
        out_shape=jax.ShapeDtypeStruct((depth, MOD_ROWS, n), F32),
        compiler_params=_cparams(2),
        name="modulation",
    )(cc, w_ada, b_ada.reshape(depth, 1, n))


def _cast_kernel(w_ref, o_ref, *, head_order_cols):
    w = w_ref[...]
    if head_order_cols:
        wh = w[:, :head_order_cols]
        quarter = (lax.broadcasted_iota(jnp.int32, wh.shape, 1) % HEAD_DIM) // ROT_PAIR
        wh = jnp.where(quarter == 1, pltpu.roll(wh, head_order_cols - ROT_PAIR, 1),
                       jnp.where(quarter == 2, pltpu.roll(wh, ROT_PAIR, 1), wh))
        w = jnp.concatenate([wh, w[:, head_order_cols:]], axis=1)
    o_ref[...] = w.astype(BF16)


def _to_bf16(w, head_order_cols=0):
    depth, rows, cols = w.shape
    total = depth * rows
    block_rows = CAST_BLOCK_BYTES // (4 * cols)
    block_rows = 1 << (block_rows.bit_length() - 1)
    assert block_rows % BF16_SUBLANES == 0 and total % block_rows == 0
    out = pl.pallas_call(
        functools.partial(_cast_kernel, head_order_cols=head_order_cols),
        grid=(total // block_rows,),
        in_specs=[pl.BlockSpec((block_rows, cols), lambda i: (i, 0))],
        out_specs=pl.BlockSpec((block_rows, cols), lambda i: (i, 0)),
        out_shape=jax.ShapeDtypeStruct((total, cols), BF16),
        compiler_params=_cparams(1),
        name="to_bf16",
    )(w.reshape(total, cols))
    return out.reshape(depth, rows, cols)


def _in_kernel(*refs, tm, tiles, rope):
    n_in = 9 if rope else 7
    x_ref, mod_ref, g1_ref, w_ref, qg_ref, kg_ref, sg_ref = refs[:7]
    cos_ref, sin_ref = refs[7:9] if rope else (None, None)
    out_refs = refs[n_in:-1]
    hs_ref = refs[-1]
    names = [nm for nm, ids in (("q", (0, 1)), ("kv", (2,)), ("pv", (3,)), ("u", (4,)), ("v", (5,)))
             if any(t in tiles for t in ids)]
    outs = dict(zip(names, out_refs))
    grp = 128

    gain = g1_ref[...] * (1.0 + mod_ref[1:2, :])
    _norm_rows(hs_ref, x_ref, tm, gain, mod_ref[0:1, :])
    hs = hs_ref[...]

    def project(i):
        return jnp.dot(hs, w_ref[:, i * IN_TILE:(i + 1) * IN_TILE], preferred_element_type=F32)

    def head(acc, r, h, g):
        xh = _rms(acc[r:r + grp, h * HEAD_DIM:(h + 1) * HEAD_DIM], g)
        if rope:
            xh = xh * cos_ref[r:r + grp, :] + pltpu.roll(xh, ROT_AXIS_DIM, 1) * sin_ref[r:r + grp, :]
        return xh.astype(BF16)

    def epilogue(t, acc):
        if t in (0, 1):
            qg = qg_ref[...] * (HEAD_DIM ** -0.5 * LOG2_E)
            for r in range(0, tm, grp):
                for h in range(IN_TILE // HEAD_DIM):
                    c0 = t * IN_TILE + h * HEAD_DIM
                    outs["q"][r:r + grp, c0:c0 + HEAD_DIM] = head(acc, r, h, qg)
        elif t == 2:
            kg = kg_ref[...]
            for r in range(0, tm, grp):
                for h in range(N_KV_HEADS):
                    outs["kv"][r:r + grp, h * HEAD_DIM:(h + 1) * HEAD_DIM] = head(acc, r, h, kg)
                outs["kv"][r:r + grp, KV_WIDTH:] = acc[r:r + grp, KV_WIDTH:].astype(BF16)
        elif t == 3:
            outs["pv"][...] = acc
        elif t == 4:
            for r in range(0, tm, grp):
                outs["u"][r:r + grp, :] = _gelu(acc[r:r + grp, :])
        else:
            sg = sg_ref[...]
            for r in range(0, tm, grp):
                outs["v"][r:r + grp, :] = _rms(_gelu(acc[r:r + grp, :]), sg).astype(BF16)

    nxt = project(0)
    for i, t in enumerate(tiles):
        acc = nxt
        if i + 1 < len(tiles):
            nxt = project(i + 1)
        epilogue(t, acc)


def _in_proj(x2, mod4, layer, mod_row, g1, w_in, qg, kg, sg, rope_tabs, *, seq, tm, tiles):
    rows, d = x2.shape
    assert rows % tm == 0 and seq % tm == 0
    tiles = tuple(tiles)
    assert tiles == tuple(range(tiles[0], tiles[0] + len(tiles))) and tiles[0] % len(tiles) == 0
    width = len(tiles) * IN_TILE
    wblk = tiles[0] // len(tiles)
    rope = rope_tabs is not None
    per_seq = seq // tm

    def mrow(m):
        return mod_row if mod_row is not None else m // per_seq

    in_specs = [
        pl.BlockSpec((tm, d), lambda m: (m, 0)),
        pl.BlockSpec((None, None, N_MOD, d), lambda m: (layer, mrow(m), 0, 0)),
        pl.BlockSpec((None, 1, d), lambda m: (layer, 0, 0)),
        _resident((None, d, width), lambda m: (layer, 0, wblk)),
        pl.BlockSpec((None, 1, HEAD_DIM), lambda m: (layer, 0, 0)),
        pl.BlockSpec((None, 1, HEAD_DIM), lambda m: (layer, 0, 0)),
        pl.BlockSpec((None, 1, SGU_WIDTH), lambda m: (layer, 0, 0)),
    ]
    args = [x2, mod4, g1, w_in, qg, kg, sg]
    if rope:
        in_specs += [pl.BlockSpec((tm, HEAD_DIM), lambda m: (m % per_seq, 0)) for _ in range(2)]
        args += list(rope_tabs)
    out_specs, out_shape = [], []

    def add(width, dtype):
        out_specs.append(pl.BlockSpec((tm, width), lambda m: (m, 0)))
        out_shape.append(jax.ShapeDtypeStruct((rows, width), dtype))

    if 0 in tiles or 1 in tiles:
        assert 0 in tiles and 1 in tiles
        add(ATTN_WIDTH, BF16)
    if 2 in tiles:
        add(IN_TILE, BF16)
    if 3 in tiles:
        add(POOL_WIDTH, F32)
    if 4 in tiles:
        add(SGU_WIDTH, F32)
    if 5 in tiles:
        add(SGU_WIDTH, BF16)
    return pl.pallas_call(
        functools.partial(_in_kernel, tm=tm, tiles=tiles, rope=rope),
        grid=(rows // tm,),
        in_specs=in_specs,
        out_specs=out_specs,
        out_shape=out_shape,
        scratch_shapes=[pltpu.VMEM((tm, d), BF16)],
        compiler_params=_cparams(1),
        name="in_proj",
    )(*args)


def _attn_unit(sink_ref, q_ref, kv_ref, kvc_ref, o_ref, qrows, kh, start, bias):
    rows = GQA_GROUP * BLOCK
    nband = 3 * BLOCK
    nt = (((1,), (1,)), ((), ()))
    q0 = kh * GQA_GROUP * HEAD_DIM
    kcol = slice(kh * HEAD_DIM, (kh + 1) * HEAD_DIM)
    vcol = slice(KV_WIDTH + kh * HEAD_DIM, KV_WIDTH + (kh + 1) * HEAD_DIM)
    qs = jnp.concatenate(
        [q_ref[qrows, q0 + g * HEAD_DIM:q0 + (g + 1) * HEAD_DIM] for g in range(GQA_GROUP)], axis=0)
    s_ctx = lax.dot_general(qs, kvc_ref[:, kcol], nt, preferred_element_type=F32)
    row = lax.broadcasted_iota(jnp.int32, (rows, 1), 0)
    sink = jnp.zeros((rows, 1), F32)
    for g in range(GQA_GROUP):
        sink = jnp.where(row // BLOCK == g, sink_ref[kh * GQA_GROUP + g] * LOG2_E, sink)
    m = jnp.maximum(jnp.max(s_ctx, axis=-1, keepdims=True), sink)
    if kv_ref is not None:
        s_band = lax.dot_general(qs, kv_ref[pl.ds(start, nband), kcol], nt,
                                 preferred_element_type=F32) + bias
        m = jnp.maximum(m, jnp.max(s_band, axis=-1, keepdims=True))
    p_ctx = jnp.exp2(s_ctx - m)
    den = jnp.sum(p_ctx, axis=-1, keepdims=True) + jnp.exp2(sink - m)
    o = jnp.dot(p_ctx.astype(BF16), kvc_ref[:, vcol], preferred_element_type=F32)
    if kv_ref is not None:
        p_band = jnp.exp2(s_band - m)
        den = den + jnp.sum(p_band, axis=-1, keepdims=True)
        o = o + jnp.dot(p_band.astype(BF16), kv_ref[pl.ds(start, nband), vcol],
                        preferred_element_type=F32)
    o = o * (1.0 / den)
    for g in range(GQA_GROUP):
        o_ref[qrows, q0 + g * HEAD_DIM:q0 + (g + 1) * HEAD_DIM] = (
            o[g * BLOCK:(g + 1) * BLOCK, :].astype(BF16))


def _attn_kernel(*refs, seq, band, qblocks):
    if band:
        sink_ref, q_ref, kv_ref, kvc_ref, o_ref = refs
    else:
        sink_ref, q_ref, kvc_ref, o_ref = refs
        kv_ref = None
    for j in range(qblocks):
        n = pl.program_id(1) * qblocks + j
        qrows = slice(j * BLOCK, (j + 1) * BLOCK)
        start = bias = None
        if band:
            nband = 3 * BLOCK
            start = pl.multiple_of(jnp.clip((n - 1) * BLOCK, 0, seq - nband), BLOCK)
            dist = (n * BLOCK - start) + (lax.broadcasted_iota(jnp.int32, (BLOCK, nband), 0)
                                          - lax.broadcasted_iota(jnp.int32, (BLOCK, nband), 1))
            bias = jnp.where(jnp.abs(dist) <= WINDOW, 0.0, -jnp.inf).astype(F32)
            bias = jnp.concatenate([bias] * GQA_GROUP, axis=0)
        for kh in range(N_KV_HEADS):
            _attn_unit(sink_ref, q_ref, kv_ref, kvc_ref, o_ref, qrows, kh, start, bias)


def _attention(sink, q, kv, kvc, *, batch, seq, ctx_len, band, qblocks):
    tq = qblocks * BLOCK
    assert seq % tq == 0
    steps = seq // tq
    in_specs = [
        pl.BlockSpec(memory_space=pltpu.SMEM),
        pl.BlockSpec((tq, ATTN_WIDTH), lambda b, n: (b * steps + n, 0)),
    ]
    args = [sink, q]
    if band:
        in_specs.append(pl.BlockSpec((seq, 2 * KV_WIDTH), lambda b, n: (b, 0)))
        args.append(kv)
    in_specs.append(pl.BlockSpec((ctx_len, 2 * KV_WIDTH), lambda b, n: (b, 0)))
    args.append(kvc)
    return pl.pallas_call(
        functools.partial(_attn_kernel, seq=seq, band=band, qblocks=qblocks),
        grid=(batch, steps),
        in_specs=in_specs,
        out_specs=pl.BlockSpec((tq, ATTN_WIDTH), lambda b, n: (b * steps + n, 0)),
        out_shape=jax.ShapeDtypeStruct((batch * seq, ATTN_WIDTH), BF16),
        compiler_params=_cparams(2),
        name="attention",
    )(*args)


def _out_kernel(attn_ref, pv_ref, pvp_ref, pvn_ref, u_ref, v_ref, x_ref, mod_ref, pw_ref, ps_ref,
                sw_ref, sbt_ref, wo_ref, o_ref, pbuf_ref, mix_ref, *, tm, seq):
    m = pl.program_id(0)
    pos0 = (m * tm) % seq
    o_ref[...] = jnp.dot(attn_ref[...], wo_ref[0:ATTN_WIDTH, :], preferred_element_type=F32)
    pbuf_ref[0:SUBLANES, :] = jnp.where(pos0 == 0, 0.0, pvp_ref[...])
    pbuf_ref[SUBLANES:SUBLANES + tm, :] = pv_ref[...]
    pbuf_ref[SUBLANES + tm:, :] = jnp.where(pos0 + tm == seq, 0.0, pvn_ref[...])
    pos = pos0 + lax.broadcasted_iota(jnp.int32, (tm, 1), 0)
    ext = tm + 2 * SUBLANES

    def ahead(a, k):
        return pltpu.roll(a, ext - k, 0)

    for gi, w in enumerate(POOL_WINDOWS):
        c0 = gi * POOL_GROUP_DIM
        cols = slice(c0, c0 + POOL_GROUP_DIM)
        run, span = pbuf_ref[:, cols], 1
        while 2 * span < w:
            run = run + ahead(run, span)
            span *= 2
        first = run[0:tm] if w // 2 == SUBLANES else ahead(run, SUBLANES - w // 2)[0:tm]
        acc = first + run[SUBLANES:SUBLANES + tm]
        lo = jnp.maximum(pos - w // 2, 0)
        hi = jnp.minimum(pos - w // 2 + w, seq)
        y = (acc * (1.0 / (hi - lo).astype(F32)) - pv_ref[:, cols]).astype(BF16)
        z = jnp.dot(y, pw_ref[gi], preferred_element_type=F32) * ps_ref[:, cols]
        mix_ref[:, cols] = z.astype(BF16)
    n_chunks = tm // SGU_CHUNK
    for h in range(N_SGU_HEADS):
        cols = slice(h * LANES, (h + 1) * LANES)
        vh = jnp.concatenate(
            [v_ref[ci * SGU_CHUNK:(ci + 1) * SGU_CHUNK, cols] for ci in range(n_chunks)], axis=1)
        r = jnp.dot(sw_ref[h], vh, preferred_element_type=F32) + sbt_ref[:, h:h + 1]
        for ci in range(n_chunks):
            rs = slice(ci * SGU_CHUNK, (ci + 1) * SGU_CHUNK)
            gated = u_ref[rs, cols] * r[:, ci * SGU_CHUNK:(ci + 1) * SGU_CHUNK]
            mix_ref[rs, POOL_WIDTH + h * LANES:POOL_WIDTH + (h + 1) * LANES] = gated.astype(BF16)
    res = o_ref[...] + jnp.dot(mix_ref[...], wo_ref[ATTN_WIDTH:, :], preferred_element_type=F32)
    o_ref[...] = x_ref[...] + mod_ref[2:3, :] * res


def _out_proj(attn, pv, u, v, x2, mod4, layer, mod_row, pool_w, pool_scale, sgu_w, sgu_bt, w_out,
              *, seq, tm):
    rows, d = x2.shape
    assert rows % tm == 0 and seq % tm == 0 and tm % SGU_CHUNK == 0
    per_seq = seq // tm
    hb = tm // SUBLANES
    last_hb = rows // SUBLANES - 1

    def mrow(m):
        return mod_row if mod_row is not None else m // per_seq

    def layer_block(a):
        shape = a.shape[1:]
        return pl.BlockSpec((None,) + shape, lambda m: (layer,) + (0,) * len(shape))

    return pl.pallas_call(
        functools.partial(_out_kernel, tm=tm, seq=seq),
        grid=(rows // tm,),
        in_specs=[
            pl.BlockSpec((tm, ATTN_WIDTH), lambda m: (m, 0)),
            pl.BlockSpec((tm, POOL_WIDTH), lambda m: (m, 0)),
            pl.BlockSpec((SUBLANES, POOL_WIDTH), lambda m: (jnp.maximum(m * hb - 1, 0), 0)),
            pl.BlockSpec((SUBLANES, POOL_WIDTH), lambda m: (jnp.minimum((m + 1) * hb, last_hb), 0)),
            pl.BlockSpec((tm, SGU_WIDTH), lambda m: (m, 0)),
            pl.BlockSpec((tm, SGU_WIDTH), lambda m: (m, 0)),
            pl.BlockSpec((tm, d), lambda m: (m, 0)),
            pl.BlockSpec((None, None, N_MOD, d), lambda m: (layer, mrow(m), 0, 0)),
            layer_block(pool_w),
            layer_block(pool_scale),
            layer_block(sgu_w),
            layer_block(sgu_bt),
            _resident((None,) + w_out.shape[1:], lambda m: (layer, 0, 0)),
        ],
        out_specs=pl.BlockSpec((tm, d), lambda m: (m, 0)),
        out_shape=jax.ShapeDtypeStruct((rows, d), F32),
        scratch_shapes=[pltpu.VMEM((tm + 2 * SUBLANES, POOL_WIDTH), F32),
                        pltpu.VMEM((tm, POOL_WIDTH + SGU_WIDTH), BF16)],
        compiler_params=_cparams(1),
        name="out_proj",
    )(attn, pv, pv, pv, u, v, x2, mod4, pool_w, pool_scale, sgu_w, sgu_bt, w_out)


def _ffn_kernel(x_hbm, xp_ref, xn_ref, mod_ref, g2_ref, wg_ref, wv_ref, cwg_ref, cwv_ref, cbg_ref,
                cbv_ref, wd_ref, fg_ref, o_ref, x_ref, x_sem, hs_ref, ag_ref, av_ref, act_ref,
                *, tm, seq, final):
    m = pl.program_id(0)
    f = pl.program_id(1)
    rows = BF16_SUBLANES
    grp = 128

    def x_copy(tile):
        return pltpu.make_async_copy(x_hbm.at[pl.ds(tile * tm, tm), :], x_ref, x_sem)

    @pl.when(f == 0)
    def _():
        @pl.when(m == 0)
        def _():
            x_copy(m).start()

        x_copy(m).wait()
        shift = mod_ref[3:4, :]
        gain = g2_ref[...] * (1.0 + mod_ref[4:5, :])
        _norm_rows(hs_ref, x_ref, tm, gain, shift, dst_offset=HALO)

        def halo(ref):
            xx = ref[...]
            inv = lax.rsqrt(jnp.mean(xx * xx, axis=-1, keepdims=True) + EPS)
            return xx * inv * gain + shift

        pos0 = (m * tm) % seq
        hn = jnp.where((pos0 + tm) % seq == 0, 0.0, halo(xn_ref))
        hp = jnp.where(pos0 == 0, 0.0, halo(xp_ref))
        pad = jnp.zeros_like(hp)
        hs_ref[0:HALO, :] = jnp.concatenate([pad, hp], axis=0).astype(BF16)
        hs_ref[HALO + tm:, :] = jnp.concatenate([hn, pad], axis=0).astype(BF16)
        o_ref[...] = x_ref[...]

    @pl.when((f == 1) & (m + 1 < pl.num_programs(0)))
    def _():
        x_copy(m + 1).start()

    hs = hs_ref[...]
    part = FF_TILE // FF_PARTS
    parts = [slice(p * part, (p + 1) * part) for p in range(FF_PARTS)]
    for cs in parts:
        ag_ref[:, cs] = jnp.dot(hs, wg_ref[:, cs], preferred_element_type=F32)
        av_ref[:, cs] = jnp.dot(hs, wv_ref[:, cs], preferred_element_type=F32)

    row_in_grp = lax.broadcasted_iota(jnp.int32, (grp, 1), 0)

    def conv(a_ref, cw_ref, cb_ref, r, cs):
        prev = a_ref[HALO - 1 + r:HALO - 1 + r + grp, cs]
        cur = a_ref[HALO + r:HALO + r + grp, cs]
        nxt = a_ref[HALO + 1 + r:HALO + 1 + r + grp, cs]
        if r > 0 and r % seq == 0:
            prev = jnp.where(row_in_grp == 0, 0.0, prev)
        if r + grp < tm and (r + grp) % seq == 0:
            nxt = jnp.where(row_in_grp == grp - 1, 0.0, nxt)
        return cb_ref[:, cs] + prev * cw_ref[0:1, cs] + cur * cw_ref[1:2, cs] + nxt * cw_ref[2:3, cs]

    for cs in parts:
        for r in range(0, tm, grp):
            act = _silu(conv(ag_ref, cwg_ref, cbg_ref, r, cs)) * conv(av_ref, cwv_ref, cbv_ref, r, cs)
            act_ref[r:r + grp, cs] = act.astype(BF16)
    gate2 = mod_ref[5:6, :]
    for r in range(0, tm, FF_DOWN_ROWS):
        rs = slice(r, min(r + FF_DOWN_ROWS, tm))
        o_ref[rs, :] += gate2 * jnp.dot(act_ref[rs, :], wd_ref[...], preferred_element_type=F32)

    if final:
        @pl.when(f == pl.num_programs(1) - 1)
        def _():
            fg = fg_ref[...]
            for r in range(0, tm, rows):
                o_ref[r:r + rows, :] = _rms(o_ref[r:r + rows, :], fg)


def _ffn(x2, mod4, layer, mod_row, g2, w_up, conv_w, conv_b, w_down, final_g, *, seq, tm, final):
    rows, d = x2.shape
    assert rows % tm == 0 and (seq % tm == 0 or tm % seq == 0) and seq % LANES == 0
    assert mod_row is not None or seq % tm == 0
    per_seq = max(seq // tm, 1)
    hb = tm // SUBLANES
    last_hb = rows // SUBLANES - 1
    nf = D_FF // FF_TILE

    def mrow(m):
        return mod_row if mod_row is not None else m // per_seq

    ext = tm + 2 * HALO
    assert nf >= 2
    return pl.pallas_call(
        functools.partial(_ffn_kernel, tm=tm, seq=seq, final=final),
        grid=(rows // tm, nf),
        in_specs=[
            pl.BlockSpec(memory_space=pl.ANY),
            pl.BlockSpec((SUBLANES, d), lambda m, f: (jnp.maximum(m * hb - 1, 0), 0)),
            pl.BlockSpec((SUBLANES, d), lambda m, f: (jnp.minimum((m + 1) * hb, last_hb), 0)),
            pl.BlockSpec((None, None, N_MOD, d), lambda m, f: (layer, mrow(m), 0, 0)),
            pl.BlockSpec((None, 1, d), lambda m, f: (layer, 0, 0)),
            pl.BlockSpec((None, d, FF_TILE), lambda m, f: (layer, 0, f)),
            pl.BlockSpec((None, d, FF_TILE), lambda m, f: (layer, 0, f + nf)),
            pl.BlockSpec((None, 3, FF_TILE), lambda m, f: (layer, 0, f)),
            pl.BlockSpec((None, 3, FF_TILE), lambda m, f: (layer, 0, f + nf)),
            pl.BlockSpec((None, 1, FF_TILE), lambda m, f: (layer, 0, f)),
            pl.BlockSpec((None, 1, FF_TILE), lambda m, f: (layer, 0, f + nf)),
            pl.BlockSpec((None, FF_TILE, d), lambda m, f: (layer, f, 0)),
            pl.BlockSpec((1, d), lambda m, f: (0, 0)),
        ],
        out_specs=pl.BlockSpec((tm, d), lambda m, f: (m, 0)),
        out_shape=jax.ShapeDtypeStruct((rows, d), F32),
        scratch_shapes=[pltpu.VMEM((tm, d), F32),
                        pltpu.SemaphoreType.DMA(()),
                        pltpu.VMEM((ext, d), BF16),
                        pltpu.VMEM((ext, FF_TILE), F32),
                        pltpu.VMEM((ext, FF_TILE), F32),
                        pltpu.VMEM((tm, FF_TILE), BF16)],
        compiler_params=_cparams(2),
        name="ffn",
    )(x2, x2, x2, mod4, g2, w_up, w_up, conv_w, conv_w, conv_b, conv_b, w_down, final_g)


def _head_lane_order(a, n_heads):
    lead = a.shape[:-1]
    a = a.reshape(lead + (n_heads, 2, 2, ROT_PAIR))
    return jnp.swapaxes(a, -2, -3).reshape(lead + (n_heads * HEAD_DIM,))


def _rope_tables(seq):
    rows = seq // GRID_W
    row_ids = jnp.repeat(jnp.arange(rows), GRID_W).astype(F32)
    col_ids = jnp.tile(jnp.arange(GRID_W), rows).astype(F32)
    inv = 1.0 / (ROPE_BASE ** (jnp.arange(0, ROT_AXIS_DIM, 2, dtype=F32) / ROT_AXIS_DIM))
    ang_r = row_ids[:, None] * inv
    ang_c = col_ids[:, None] * inv
    cr, sr, cc, sc = jnp.cos(ang_r), jnp.sin(ang_r), jnp.cos(ang_c), jnp.sin(ang_c)
    cos = jnp.concatenate([cr, cc, cr, cc], axis=-1)
    sin_signed = jnp.concatenate([-sr, -sc, sr, sc], axis=-1)
    return cos, sin_signed


def kernel(x, c, ctx, c_ctx, norm1_g, norm2_g, w_ada, b_ada, w_in, q_norm_g, k_norm_g, attn_sink,
           pool_w, pool_scale, sgu_norm_g, sgu_w, sgu_b, w_out, w_up, conv_w, conv_b, w_down,
           final_norm_g):
    batch, seq, d = x.shape
    ctx_len = ctx.shape[1]
    depth = w_ada.shape[0]
    assert d == D_MODEL and batch + 1 <= MOD_ROWS
    ctx_row = batch

    cc = jnp.zeros((MOD_ROWS, d), F32).at[:batch].set(c).at[ctx_row].set(c_ctx)
    mod4 = _modulation(cc, w_ada, b_ada).reshape(depth, MOD_ROWS, N_MOD, d)
    rope_tabs = _rope_tables(seq)

    w_in_b = _to_bf16(w_in, head_order_cols=ATTN_WIDTH + KV_WIDTH)
    w_out_b, w_up_b, w_down_b = (_to_bf16(w) for w in (w_out, w_up, w_down))
    pool_w_b, sgu_w_b = pool_w.astype(BF16), sgu_w.astype(BF16)
    g1, g2 = norm1_g[:, None], norm2_g[:, None]
    qg = _head_lane_order(q_norm_g, 1)[:, None]
    kg = _head_lane_order(k_norm_g, 1)[:, None]
    sg = sgu_norm_g[:, None]
    mixer = (pool_w_b, pool_scale[:, None], sgu_w_b, jnp.swapaxes(sgu_b, 1, 2), w_out_b)
    mlp = (g2, w_up_b, conv_w, conv_b[:, None], w_down_b, final_norm_g[None])

    x2 = x.reshape(batch * seq, d)
    xc2 = ctx.reshape(batch * ctx_len, d)
    lat_tm = 512
    in_tm = 512
    ctx_tm = ctx_len
    ffn_tm = 1024
    ctx_ffn_tm = 1024
    attn_qblocks = 4
    for l in range(depth):
        last = l == depth - 1
        q, kv, pv, u, v = _in_proj(x2, mod4, l, None, g1, w_in_b, qg, kg, sg, rope_tabs,
                                   seq=seq, tm=in_tm, tiles=range(6))
        if last:
            (kvc,) = _in_proj(xc2, mod4, l, ctx_row, g1, w_in_b, qg, kg, sg, None,
                              seq=ctx_len, tm=ctx_tm, tiles=(2,))
        else:
            qc, kvc, pvc, uc, vc = _in_proj(xc2, mod4, l, ctx_row, g1, w_in_b, qg, kg, sg, None,
                                            seq=ctx_len, tm=ctx_tm, tiles=range(6))
        attn = _attention(attn_sink[l], q, kv, kvc, batch=batch, seq=seq, ctx_len=ctx_len, band=True,
                          qblocks=attn_qblocks)
        x2 = _out_proj(attn, pv, u, v, x2, mod4, l, None, *mixer, seq=seq, tm=lat_tm)
        if not last:
            attn_c = _attention(attn_sink[l], qc, None, kvc, batch=batch, seq=ctx_len,
                                ctx_len=ctx_len, band=False,
                                qblocks=min(attn_qblocks, ctx_len // BLOCK))
            xc2 = _out_proj(attn_c, pvc, uc, vc, xc2, mod4, l, ctx_row, *mixer, seq=ctx_len, tm=ctx_tm)
        x2 = _ffn(x2, mod4, l, None, *mlp, seq=seq, tm=ffn_tm, final=last)
        if not last:
            xc2 = _ffn(xc2, mod4, l, ctx_row, *mlp, seq=ctx_len, tm=ctx_ffn_tm, final=False)
    return x2.reshape(batch, seq, d)
```

```python
import functools

import jax
import jax.numpy as jnp
from jax import lax
from jax.experimental import pallas as pl
from jax.experimental.pallas import tpu as pltpu

F32 = jnp.float32
BF16 = jnp.bfloat16

D_MODEL = 2048
GRID_W = 64
HEAD_DIM = 128
N_Q_HEADS = 8
N_KV_HEADS = 2
GQA_GROUP = N_Q_HEADS // N_KV_HEADS
ATTN_WIDTH = N_Q_HEADS * HEAD_DIM
KV_WIDTH = N_KV_HEADS * HEAD_DIM
WINDOW = 128
BLOCK = 128
ROPE_BASE = 10000.0
ROT_AXIS_DIM = HEAD_DIM // 2
ROT_PAIR = ROT_AXIS_DIM // 2
POOL_WINDOWS = (2, 4, 8, 16)
POOL_WIDTH = 512
POOL_GROUP_DIM = 128
SGU_WIDTH = 512
N_SGU_HEADS = 4
SGU_CHUNK = 128
IN_WIDTH = 3072
D_FF = 5632
N_MOD = 6
EPS = 1e-6
LOG2_E = 1.4426950408889634

SUBLANES = 8
BF16_SUBLANES = 16
LANES = 128
HALO = BF16_SUBLANES
MOD_ROWS = 16
IN_TILE = 512
FF_TILE = 512
FF_PARTS = 2
FF_DOWN_ROWS = 512
VMEM_LIMIT = 60 * 1024 * 1024
CAST_BLOCK_BYTES = 6 * 1024 * 1024


def _cparams(n_axes):
    return pltpu.CompilerParams(
        dimension_semantics=("arbitrary",) * n_axes, vmem_limit_bytes=VMEM_LIMIT)


def _rms(x, g):
    return x * lax.rsqrt(jnp.mean(x * x, axis=-1, keepdims=True) + EPS) * g


def _gelu(x):
    return 0.5 * x * (1.0 + lax.erf(x * 0.7071067811865476))


def _silu(x):
    return x * jax.nn.sigmoid(x)


def _norm_rows(dst_ref, x_ref, n_rows, gain, shift, *, dst_offset=0):
    rows = BF16_SUBLANES
    for r in range(0, n_rows, rows):
        xx = x_ref[r:r + rows, :]
        inv = lax.rsqrt(jnp.mean(xx * xx, axis=-1, keepdims=True) + EPS)
        dst_ref[dst_offset + r:dst_offset + r + rows, :] = (xx * inv * gain + shift).astype(BF16)


def _resident(shape, index_map):
    return pl.BlockSpec(shape, index_map, pipeline_mode=pl.Buffered(1))


def _mod_kernel(c_ref, w_ref, b_ref, o_ref):
    c = c_ref[...]
    s = _silu(c).astype(BF16)
    o_ref[...] = jnp.dot(s, w_ref[...].astype(BF16), preferred_element_type=F32) + b_ref[...]


def _modulation(cc, w_ada, b_ada):
    depth, d, n = w_ada.shape
    tn = 1024
    return pl.pallas_call(
        _mod_kernel,
        grid=(depth, n // tn),
        in_specs=[
            pl.BlockSpec((MOD_ROWS, d), lambda l, j: (0, 0)),
            pl.BlockSpec((None, d, tn), lambda l, j: (l, 0, j)),
            pl.BlockSpec((None, 1, tn), lambda l, j: (l, 0, j)),
        ],
        out_specs=pl.BlockSpec((None, MOD_ROWS, tn), lambda l, j: (l, 0, j)),
        out_shape=jax.ShapeDtypeStruct((depth, MOD_ROWS, n), F32),
        compiler_params=_cparams(2),
        name="modulation",
    )(cc, w_ada, b_ada.reshape(depth, 1, n))


def _cast_kernel(w_ref, o_ref, *, head_order_cols):
    w = w_ref[...]
    if head_order_cols:
        wh = w[:, :head_order_cols]
        quarter = (lax.broadcasted_iota(jnp.int32, wh.shape, 1) % HEAD_DIM) // ROT_PAIR
        wh = jnp.where(quarter == 1, pltpu.roll(wh, head_order_cols - ROT_PAIR, 1),
                       jnp.where(quarter == 2, pltpu.roll(wh, ROT_PAIR, 1), wh))
        w = jnp.concatenate([wh, w[:, head_order_cols:]], axis=1)
    o_ref[...] = w.astype(BF16)


def _to_bf16(w, head_order_cols=0):
    depth, rows, cols = w.shape
    total = depth * rows
    block_rows = CAST_BLOCK_BYTES // (4 * cols)
    block_rows = 1 << (block_rows.bit_length() - 1)
    assert block_rows % BF16_SUBLANES == 0 and total % block_rows == 0
    out = pl.pallas_call(
        functools.partial(_cast_kernel, head_order_cols=head_order_cols),
        grid=(total // block_rows,),
        in_specs=[pl.BlockSpec((block_rows, cols), lambda i: (i, 0))],
        out_specs=pl.BlockSpec((block_rows, cols), lambda i: (i, 0)),
        out_shape=jax.ShapeDtypeStruct((total, cols), BF16),
        compiler_params=_cparams(1),
        name="to_bf16",
    )(w.reshape(total, cols))
    return out.reshape(depth, rows, cols)


def _in_kernel(*refs, tm, tiles, rope):
    n_in = 9 if rope else 7
    x_ref, mod_ref, g1_ref, w_ref, qg_ref, kg_ref, sg_ref = refs[:7]
    cos_ref, sin_ref = refs[7:9] if rope else (None, None)
    out_refs = refs[n_in:-1]
    hs_ref = refs[-1]
    names = [nm for nm, ids in (("q", (0, 1)), ("kv", (2,)), ("pv", (3,)), ("u", (4,)), ("v", (5,)))
             if any(t in tiles for t in ids)]
    outs = dict(zip(names, out_refs))
    grp = 128

    gain = g1_ref[...] * (1.0 + mod_ref[1:2, :])
    _norm_rows(hs_ref, x_ref, tm, gain, mod_ref[0:1, :])
    hs = hs_ref[...]

    def project(i):
        return jnp.dot(hs, w_ref[:, i * IN_TILE:(i + 1) * IN_TILE], preferred_element_type=F32)

    def head(acc, r, h, g):
        xh = _rms(acc[r:r + grp, h * HEAD_DIM:(h + 1) * HEAD_DIM], g)
        if rope:
            xh = xh * cos_ref[r:r + grp, :] + pltpu.roll(xh, ROT_AXIS_DIM, 1) * sin_ref[r:r + grp, :]
        return xh.astype(BF16)

    def epilogue(t, acc):
        if t in (0, 1):
            qg = qg_ref[...] * (HEAD_DIM ** -0.5 * LOG2_E)
            for r in range(0, tm, grp):
                for h in range(IN_TILE // HEAD_DIM):
                    c0 = t * IN_TILE + h * HEAD_DIM
                    outs["q"][r:r + grp, c0:c0 + HEAD_DIM] = head(acc, r, h, qg)
        elif t == 2:
            kg = kg_ref[...]
            for r in range(0, tm, grp):
                for h in range(N_KV_HEADS):
                    outs["kv"][r:r + grp, h * HEAD_DIM:(h + 1) * HEAD_DIM] = head(acc, r, h, kg)
                outs["kv"][r:r + grp, KV_WIDTH:] = acc[r:r + grp, KV_WIDTH:].astype(BF16)
        elif t == 3:
            outs["pv"][...] = acc
        elif t == 4:
            for r in range(0, tm, grp):
                outs["u"][r:r + grp, :] = _gelu(acc[r:r + grp, :])
        else:
            sg = sg_ref[...]
            for r in range(0, tm, grp):
                outs["v"][r:r + grp, :] = _rms(_gelu(acc[r:r + grp, :]), sg).astype(BF16)

    nxt = project(0)
    for i, t in enumerate(tiles):
        acc = nxt
        if i + 1 < len(tiles):
            nxt = project(i + 1)
        epilogue(t, acc)


def _in_proj(x2, mod4, layer, mod_row, g1, w_in, qg, kg, sg, rope_tabs, *, seq, tm, tiles):
    rows, d = x2.shape
    rope = rope_tabs is not None
    assert rows % tm == 0 and (seq % tm == 0 or (mod_row is not None and not rope))
    tiles = tuple(tiles)
    assert tiles == tuple(range(tiles[0], tiles[0] + len(tiles))) and tiles[0] % len(tiles) == 0
    width = len(tiles) * IN_TILE
    wblk = tiles[0] // len(tiles)
    per_seq = max(seq // tm, 1)

    def mrow(m):
        return mod_row if mod_row is not None else m // per_seq

    in_specs = [
        pl.BlockSpec((tm, d), lambda m: (m, 0)),
        pl.BlockSpec((None, None, N_MOD, d), lambda m: (layer, mrow(m), 0, 0)),
        pl.BlockSpec((None, 1, d), lambda m: (layer, 0, 0)),
        _resident((None, d, width), lambda m: (layer, 0, wblk)),
        pl.BlockSpec((None, 1, HEAD_DIM), lambda m: (layer, 0, 0)),
        pl.BlockSpec((None, 1, HEAD_DIM), lambda m: (layer, 0, 0)),
        pl.BlockSpec((None, 1, SGU_WIDTH), lambda m: (layer, 0, 0)),
    ]
    args = [x2, mod4, g1, w_in, qg, kg, sg]
    if rope:
        in_specs += [pl.BlockSpec((tm, HEAD_DIM), lambda m: (m % per_seq, 0)) for _ in range(2)]
        args += list(rope_tabs)
    out_specs, out_shape = [], []

    def add(width, dtype):
        out_specs.append(pl.BlockSpec((tm, width), lambda m: (m, 0)))
        out_shape.append(jax.ShapeDtypeStruct((rows, width), dtype))

    if 0 in tiles or 1 in tiles:
        assert 0 in tiles and 1 in tiles
        add(ATTN_WIDTH, BF16)
    if 2 in tiles:
        add(IN_TILE, BF16)
    if 3 in tiles:
        add(POOL_WIDTH, F32)
    if 4 in tiles:
        add(SGU_WIDTH, F32)
    if 5 in tiles:
        add(SGU_WIDTH, BF16)
    return pl.pallas_call(
        functools.partial(_in_kernel, tm=tm, tiles=tiles, rope=rope),
        grid=(rows // tm,),
        in_specs=in_specs,
        out_specs=out_specs,
        out_shape=out_shape,
        scratch_shapes=[pltpu.VMEM((tm, d), BF16)],
        compiler_params=_cparams(1),
        name="in_proj",
    )(*args)


def _attn_unit(sink_ref, q_ref, kv_ref, kvc_ref, o_ref, qrows, kh, start, bias):
    rows = GQA_GROUP * BLOCK
    nband = 3 * BLOCK
    nt = (((1,), (1,)), ((), ()))
    q0 = kh * GQA_GROUP * HEAD_DIM
    kcol = slice(kh * HEAD_DIM, (kh + 1) * HEAD_DIM)
    vcol = slice(KV_WIDTH + kh * HEAD_DIM, KV_WIDTH + (kh + 1) * HEAD_DIM)
    qs = jnp.concatenate(
        [q_ref[qrows, q0 + g * HEAD_DIM:q0 + (g + 1) * HEAD_DIM] for g in range(GQA_GROUP)], axis=0)
    s_ctx = lax.dot_general(qs, kvc_ref[:, kcol], nt, preferred_element_type=F32)
    row = lax.broadcasted_iota(jnp.int32, (rows, 1), 0)
    sink = jnp.zeros((rows, 1), F32)
    for g in range(GQA_GROUP):
        sink = jnp.where(row // BLOCK == g, sink_ref[kh * GQA_GROUP + g] * LOG2_E, sink)
    m = jnp.maximum(jnp.max(s_ctx, axis=-1, keepdims=True), sink)
    if kv_ref is not None:
        s_band = lax.dot_general(qs, kv_ref[pl.ds(start, nband), kcol], nt,
                                 preferred_element_type=F32) + bias
        m = jnp.maximum(m, jnp.max(s_band, axis=-1, keepdims=True))
    p_ctx = jnp.exp2(s_ctx - m)
    den = jnp.sum(p_ctx, axis=-1, keepdims=True) + jnp.exp2(sink - m)
    o = jnp.dot(p_ctx.astype(BF16), kvc_ref[:, vcol], preferred_element_type=F32)
    if kv_ref is not None:
        p_band = jnp.exp2(s_band - m)
        den = den + jnp.sum(p_band, axis=-1, keepdims=True)
        o = o + jnp.dot(p_band.astype(BF16), kv_ref[pl.ds(start, nband), vcol],
                        preferred_element_type=F32)
    o = o * (1.0 / den)
    for g in range(GQA_GROUP):
        o_ref[qrows, q0 + g * HEAD_DIM:q0 + (g + 1) * HEAD_DIM] = (
            o[g * BLOCK:(g + 1) * BLOCK, :].astype(BF16))


def _attn_kernel(*refs, seq, band, qblocks):
    if band:
        sink_ref, q_ref, kv_ref, kvc_ref, o_ref = refs
    else:
        sink_ref, q_ref, kvc_ref, o_ref = refs
        kv_ref = None
    for j in range(qblocks):
        n = pl.program_id(1) * qblocks + j
        qrows = slice(j * BLOCK, (j + 1) * BLOCK)
        start = bias = None
        if band:
            nband = 3 * BLOCK
            start = pl.multiple_of(jnp.clip((n - 1) * BLOCK, 0, seq - nband), BLOCK)
            dist = (n * BLOCK - start) + (lax.broadcasted_iota(jnp.int32, (BLOCK, nband), 0)
                                          - lax.broadcasted_iota(jnp.int32, (BLOCK, nband), 1))
            bias = jnp.where(jnp.abs(dist) <= WINDOW, 0.0, -jnp.inf).astype(F32)
            bias = jnp.concatenate([bias] * GQA_GROUP, axis=0)
        for kh in range(N_KV_HEADS):
            _attn_unit(sink_ref, q_ref, kv_ref, kvc_ref, o_ref, qrows, kh, start, bias)


def _attention(sink, q, kv, kvc, *, batch, seq, ctx_len, band, qblocks):
    tq = qblocks * BLOCK
    assert seq % tq == 0
    steps = seq // tq
    in_specs = [
        pl.BlockSpec(memory_space=pltpu.SMEM),
        pl.BlockSpec((tq, ATTN_WIDTH), lambda b, n: (b * steps + n, 0)),
    ]
    args = [sink, q]
    if band:
        in_specs.append(pl.BlockSpec((seq, 2 * KV_WIDTH), lambda b, n: (b, 0)))
        args.append(kv)
    in_specs.append(pl.BlockSpec((ctx_len, 2 * KV_WIDTH), lambda b, n: (b, 0)))
    args.append(kvc)
    return pl.pallas_call(
        functools.partial(_attn_kernel, seq=seq, band=band, qblocks=qblocks),
        grid=(batch, steps),
        in_specs=in_specs,
        out_specs=pl.BlockSpec((tq, ATTN_WIDTH), lambda b, n: (b * steps + n, 0)),
        out_shape=jax.ShapeDtypeStruct((batch * seq, ATTN_WIDTH), BF16),
        compiler_params=_cparams(2),
        name="attention",
    )(*args)


def _out_kernel(attn_ref, pv_ref, pvp_ref, pvn_ref, u_ref, v_ref, x_ref, mod_ref, pw_ref, ps_ref,
                sw_ref, sbt_ref, wo_ref, o_ref, pbuf_ref, mix_ref, *, tm, seq):
    m = pl.program_id(0)
    pos0 = (m * tm) % seq
    o_ref[...] = jnp.dot(attn_ref[...], wo_ref[0:ATTN_WIDTH, :], preferred_element_type=F32)
    pbuf_ref[0:SUBLANES, :] = jnp.where(pos0 == 0, 0.0, pvp_ref[...])
    pbuf_ref[SUBLANES:SUBLANES + tm, :] = pv_ref[...]
    pbuf_ref[SUBLANES + tm:, :] = jnp.where(pos0 + tm == seq, 0.0, pvn_ref[...])
    pos = pos0 + lax.broadcasted_iota(jnp.int32, (tm, 1), 0)
    ext = tm + 2 * SUBLANES

    def ahead(a, k):
        return pltpu.roll(a, ext - k, 0)

    for gi, w in enumerate(POOL_WINDOWS):
        c0 = gi * POOL_GROUP_DIM
        cols = slice(c0, c0 + POOL_GROUP_DIM)
        run, span = pbuf_ref[:, cols], 1
        while 2 * span < w:
            run = run + ahead(run, span)
            span *= 2
        first = run[0:tm] if w // 2 == SUBLANES else ahead(run, SUBLANES - w // 2)[0:tm]
        acc = first + run[SUBLANES:SUBLANES + tm]
        lo = jnp.maximum(pos - w // 2, 0)
        hi = jnp.minimum(pos - w // 2 + w, seq)
        y = (acc * (1.0 / (hi - lo).astype(F32)) - pv_ref[:, cols]).astype(BF16)
        z = jnp.dot(y, pw_ref[gi], preferred_element_type=F32) * ps_ref[:, cols]
        mix_ref[:, cols] = z.astype(BF16)
    n_chunks = tm // SGU_CHUNK
    for h in range(N_SGU_HEADS):
        cols = slice(h * LANES, (h + 1) * LANES)
        vh = jnp.concatenate(
            [v_ref[ci * SGU_CHUNK:(ci + 1) * SGU_CHUNK, cols] for ci in range(n_chunks)], axis=1)
        r = jnp.dot(sw_ref[h], vh, preferred_element_type=F32) + sbt_ref[:, h:h + 1]
        for ci in range(n_chunks):
            rs = slice(ci * SGU_CHUNK, (ci + 1) * SGU_CHUNK)
            gated = u_ref[rs, cols] * r[:, ci * SGU_CHUNK:(ci + 1) * SGU_CHUNK]
            mix_ref[rs, POOL_WIDTH + h * LANES:POOL_WIDTH + (h + 1) * LANES] = gated.astype(BF16)
    res = o_ref[...] + jnp.dot(mix_ref[...], wo_ref[ATTN_WIDTH:, :], preferred_element_type=F32)
    o_ref[...] = x_ref[...] + mod_ref[2:3, :] * res


def _out_proj(attn, pv, u, v, x2, mod4, layer, mod_row, pool_w, pool_scale, sgu_w, sgu_bt, w_out,
              *, seq, tm):
    rows, d = x2.shape
    assert rows % tm == 0 and seq % tm == 0 and tm % SGU_CHUNK == 0
    per_seq = seq // tm
    hb = tm // SUBLANES
    last_hb = rows // SUBLANES - 1

    def mrow(m):
        return mod_row if mod_row is not None else m // per_seq

    def layer_block(a):
        shape = a.shape[1:]
        return pl.BlockSpec((None,) + shape, lambda m: (layer,) + (0,) * len(shape))

    return pl.pallas_call(
        functools.partial(_out_kernel, tm=tm, seq=seq),
        grid=(rows // tm,),
        in_specs=[
            pl.BlockSpec((tm, ATTN_WIDTH), lambda m: (m, 0)),
            pl.BlockSpec((tm, POOL_WIDTH), lambda m: (m, 0)),
            pl.BlockSpec((SUBLANES, POOL_WIDTH), lambda m: (jnp.maximum(m * hb - 1, 0), 0)),
            pl.BlockSpec((SUBLANES, POOL_WIDTH), lambda m: (jnp.minimum((m + 1) * hb, last_hb), 0)),
            pl.BlockSpec((tm, SGU_WIDTH), lambda m: (m, 0)),
            pl.BlockSpec((tm, SGU_WIDTH), lambda m: (m, 0)),
            pl.BlockSpec((tm, d), lambda m: (m, 0)),
            pl.BlockSpec((None, None, N_MOD, d), lambda m: (layer, mrow(m), 0, 0)),
            layer_block(pool_w),
            layer_block(pool_scale),
            layer_block(sgu_w),
            layer_block(sgu_bt),
            _resident((None,) + w_out.shape[1:], lambda m: (layer, 0, 0)),
        ],
        out_specs=pl.BlockSpec((tm, d), lambda m: (m, 0)),
        out_shape=jax.ShapeDtypeStruct((rows, d), F32),
        scratch_shapes=[pltpu.VMEM((tm + 2 * SUBLANES, POOL_WIDTH), F32),
                        pltpu.VMEM((tm, POOL_WIDTH + SGU_WIDTH), BF16)],
        compiler_params=_cparams(1),
        name="out_proj",
    )(attn, pv, pv, pv, u, v, x2, mod4, pool_w, pool_scale, sgu_w, sgu_bt, w_out)


def _ffn_kernel(x_hbm, xp_ref, xn_ref, mod_ref, g2_ref, wg_ref, wv_ref, cwg_ref, cwv_ref, cbg_ref,
                cbv_ref, wd_ref, fg_ref, o_ref, x_ref, x_sem, hs_ref, ag_ref, av_ref, act_ref,
                *, tm, seq, final):
    m = pl.program_id(0)
    f = pl.program_id(1)
    rows = BF16_SUBLANES
    grp = 128

    def x_copy(tile):
        return pltpu.make_async_copy(x_hbm.at[pl.ds(tile * tm, tm), :], x_ref, x_sem)

    @pl.when(f == 0)
    def _():
        @pl.when(m == 0)
        def _():
            x_copy(m).start()

        x_copy(m).wait()
        shift = mod_ref[3:4, :]
        gain = g2_ref[...] * (1.0 + mod_ref[4:5, :])
        _norm_rows(hs_ref, x_ref, tm, gain, shift, dst_offset=HALO)

        def halo(ref):
            xx = ref[...]
            inv = lax.rsqrt(jnp.mean(xx * xx, axis=-1, keepdims=True) + EPS)
            return xx * inv * gain + shift

        pos0 = (m * tm) % seq
        hn = jnp.where((pos0 + tm) % seq == 0, 0.0, halo(xn_ref))
        hp = jnp.where(pos0 == 0, 0.0, halo(xp_ref))
        pad = jnp.zeros_like(hp)
        hs_ref[0:HALO, :] = jnp.concatenate([pad, hp], axis=0).astype(BF16)
        hs_ref[HALO + tm:, :] = jnp.concatenate([hn, pad], axis=0).astype(BF16)
        o_ref[...] = x_ref[...]

    @pl.when((f == 1) & (m + 1 < pl.num_programs(0)))
    def _():
        x_copy(m + 1).start()

    hs = hs_ref[...]
    part = FF_TILE // FF_PARTS
    parts = [slice(p * part, (p + 1) * part) for p in range(FF_PARTS)]
    for cs in parts:
        ag_ref[:, cs] = jnp.dot(hs, wg_ref[:, cs], preferred_element_type=F32)
        av_ref[:, cs] = jnp.dot(hs, wv_ref[:, cs], preferred_element_type=F32)

    row_in_grp = lax.broadcasted_iota(jnp.int32, (grp, 1), 0)

    def conv(a_ref, cw_ref, cb_ref, r, cs):
        prev = a_ref[HALO - 1 + r:HALO - 1 + r + grp, cs]
        cur = a_ref[HALO + r:HALO + r + grp, cs]
        nxt = a_ref[HALO + 1 + r:HALO + 1 + r + grp, cs]
        if r > 0 and r % seq == 0:
            prev = jnp.where(row_in_grp == 0, 0.0, prev)
        if r + grp < tm and (r + grp) % seq == 0:
            nxt = jnp.where(row_in_grp == grp - 1, 0.0, nxt)
        return cb_ref[:, cs] + prev * cw_ref[0:1, cs] + cur * cw_ref[1:2, cs] + nxt * cw_ref[2:3, cs]

    for cs in parts:
        for r in range(0, tm, grp):
            act = _silu(conv(ag_ref, cwg_ref, cbg_ref, r, cs)) * conv(av_ref, cwv_ref, cbv_ref, r, cs)
            act_ref[r:r + grp, cs] = act.astype(BF16)
    gate2 = mod_ref[5:6, :]
    for r in range(0, tm, FF_DOWN_ROWS):
        rs = slice(r, min(r + FF_DOWN_ROWS, tm))
        o_ref[rs, :] += gate2 * jnp.dot(act_ref[rs, :], wd_ref[...], preferred_element_type=F32)

    if final:
        @pl.when(f == pl.num_programs(1) - 1)
        def _():
            fg = fg_ref[...]
            for r in range(0, tm, rows):
                o_ref[r:r + rows, :] = _rms(o_ref[r:r + rows, :], fg)


def _ffn(x2, mod4, layer, mod_row, g2, w_up, conv_w, conv_b, w_down, final_g, *, seq, tm, final):
    rows, d = x2.shape
    assert rows % tm == 0 and (seq % tm == 0 or tm % seq == 0) and seq % LANES == 0
    assert mod_row is not None or seq % tm == 0
    per_seq = max(seq // tm, 1)
    hb = tm // SUBLANES
    last_hb = rows // SUBLANES - 1
    nf = D_FF // FF_TILE

    def mrow(m):
        return mod_row if mod_row is not None else m // per_seq

    ext = tm + 2 * HALO
    assert nf >= 2
    return pl.pallas_call(
        functools.partial(_ffn_kernel, tm=tm, seq=seq, final=final),
        grid=(rows // tm, nf),
        in_specs=[
            pl.BlockSpec(memory_space=pl.ANY),
            pl.BlockSpec((SUBLANES, d), lambda m, f: (jnp.maximum(m * hb - 1, 0), 0)),
            pl.BlockSpec((SUBLANES, d), lambda m, f: (jnp.minimum((m + 1) * hb, last_hb), 0)),
            pl.BlockSpec((None, None, N_MOD, d), lambda m, f: (layer, mrow(m), 0, 0)),
            pl.BlockSpec((None, 1, d), lambda m, f: (layer, 0, 0)),
            pl.BlockSpec((None, d, FF_TILE), lambda m, f: (layer, 0, f)),
            pl.BlockSpec((None, d, FF_TILE), lambda m, f: (layer, 0, f + nf)),
            pl.BlockSpec((None, 3, FF_TILE), lambda m, f: (layer, 0, f)),
            pl.BlockSpec((None, 3, FF_TILE), lambda m, f: (layer, 0, f + nf)),
            pl.BlockSpec((None, 1, FF_TILE), lambda m, f: (layer, 0, f)),
            pl.BlockSpec((None, 1, FF_TILE), lambda m, f: (layer, 0, f + nf)),
            pl.BlockSpec((None, FF_TILE, d), lambda m, f: (layer, f, 0)),
            pl.BlockSpec((1, d), lambda m, f: (0, 0)),
        ],
        out_specs=pl.BlockSpec((tm, d), lambda m, f: (m, 0)),
        out_shape=jax.ShapeDtypeStruct((rows, d), F32),
        scratch_shapes=[pltpu.VMEM((tm, d), F32),
                        pltpu.SemaphoreType.DMA(()),
                        pltpu.VMEM((ext, d), BF16),
                        pltpu.VMEM((ext, FF_TILE), F32),
                        pltpu.VMEM((ext, FF_TILE), F32),
                        pltpu.VMEM((tm, FF_TILE), BF16)],
        compiler_params=_cparams(2),
        name="ffn",
    )(x2, x2, x2, mod4, g2, w_up, w_up, conv_w, conv_w, conv_b, conv_b, w_down, final_g)


def _head_lane_order(a, n_heads):
    lead = a.shape[:-1]
    a = a.reshape(lead + (n_heads, 2, 2, ROT_PAIR))
    return jnp.swapaxes(a, -2, -3).reshape(lead + (n_heads * HEAD_DIM,))


def _rope_tables(seq):
    rows = seq // GRID_W
    row_ids = jnp.repeat(jnp.arange(rows), GRID_W).astype(F32)
    col_ids = jnp.tile(jnp.arange(GRID_W), rows).astype(F32)
    inv = 1.0 / (ROPE_BASE ** (jnp.arange(0, ROT_AXIS_DIM, 2, dtype=F32) / ROT_AXIS_DIM))
    ang_r = row_ids[:, None] * inv
    ang_c = col_ids[:, None] * inv
    cr, sr, cc, sc = jnp.cos(ang_r), jnp.sin(ang_r), jnp.cos(ang_c), jnp.sin(ang_c)
    cos = jnp.concatenate([cr, cc, cr, cc], axis=-1)
    sin_signed = jnp.concatenate([-sr, -sc, sr, sc], axis=-1)
    return cos, sin_signed


def kernel(x, c, ctx, c_ctx, norm1_g, norm2_g, w_ada, b_ada, w_in, q_norm_g, k_norm_g, attn_sink,
           pool_w, pool_scale, sgu_norm_g, sgu_w, sgu_b, w_out, w_up, conv_w, conv_b, w_down,
           final_norm_g):
    batch, seq, d = x.shape
    ctx_len = ctx.shape[1]
    depth = w_ada.shape[0]
    assert d == D_MODEL and batch + 1 <= MOD_ROWS
    ctx_row = batch

    cc = jnp.zeros((MOD_ROWS, d), F32).at[:batch].set(c).at[ctx_row].set(c_ctx)
    mod4 = _modulation(cc, w_ada, b_ada).reshape(depth, MOD_ROWS, N_MOD, d)
    rope_tabs = _rope_tables(seq)

    w_in_b = _to_bf16(w_in, head_order_cols=ATTN_WIDTH + KV_WIDTH)
    w_out_b, w_up_b, w_down_b = (_to_bf16(w) for w in (w_out, w_up, w_down))
    pool_w_b, sgu_w_b = pool_w.astype(BF16), sgu_w.astype(BF16)
    g1, g2 = norm1_g[:, None], norm2_g[:, None]
    qg = _head_lane_order(q_norm_g, 1)[:, None]
    kg = _head_lane_order(k_norm_g, 1)[:, None]
    sg = sgu_norm_g[:, None]
    mixer = (pool_w_b, pool_scale[:, None], sgu_w_b, jnp.swapaxes(sgu_b, 1, 2), w_out_b)
    mlp = (g2, w_up_b, conv_w, conv_b[:, None], w_down_b, final_norm_g[None])

    x2 = x.reshape(batch * seq, d)
    xc2 = ctx.reshape(batch * ctx_len, d)
    lat_tm = 512
    in_tm = 1024
    ctx_tm = ctx_len
    ffn_tm = 1024
    ctx_ffn_tm = 1024
    attn_qblocks = 4
    for l in range(depth):
        last = l == depth - 1
        q, kv, pv, u, v = _in_proj(x2, mod4, l, None, g1, w_in_b, qg, kg, sg, rope_tabs,
                                   seq=seq, tm=in_tm, tiles=range(6))
        if last:
            (kvc,) = _in_proj(xc2, mod4, l, ctx_row, g1, w_in_b, qg, kg, sg, None,
                              seq=ctx_len, tm=in_tm, tiles=(2,))
        else:
            qc, kvc, pvc, uc, vc = _in_proj(xc2, mod4, l, ctx_row, g1, w_in_b, qg, kg, sg, None,
                                            seq=ctx_len, tm=in_tm, tiles=range(6))
        attn = _attention(attn_sink[l], q, kv, kvc, batch=batch, seq=seq, ctx_len=ctx_len, band=True,
                          qblocks=attn_qblocks)
        x2 = _out_proj(attn, pv, u, v, x2, mod4, l, None, *mixer, seq=seq, tm=lat_tm)
        if not last:
            attn_c = _attention(attn_sink[l], qc, None, kvc, batch=batch, seq=ctx_len,
                                ctx_len=ctx_len, band=False,
                                qblocks=min(attn_qblocks, ctx_len // BLOCK))
            xc2 = _out_proj(attn_c, pvc, uc, vc, xc2, mod4, l, ctx_row, *mixer, seq=ctx_len, tm=ctx_tm)
        x2 = _ffn(x2, mod4, l, None, *mlp, seq=seq, tm=ffn_tm, final=last)
        if not last:
            xc2 = _ffn(xc2, mod4, l, ctx_row, *mlp, seq=ctx_len, tm=ctx_ffn_tm, final=False)
    return x2.reshape(batch, seq, d)
```

```python
import functools

import jax
import jax.numpy as jnp
from jax import lax
from jax.experimental import pallas as pl
from jax.experimental.pallas import tpu as pltpu

F32 = jnp.float32
BF16 = jnp.bfloat16

D_MODEL = 2048
GRID_W = 64
HEAD_DIM = 128
N_Q_HEADS = 8
N_KV_HEADS = 2
GQA_GROUP = N_Q_HEADS // N_KV_HEADS
ATTN_WIDTH = N_Q_HEADS * HEAD_DIM
KV_WIDTH = N_KV_HEADS * HEAD_DIM
WINDOW = 128
BLOCK = 128
ROPE_BASE = 10000.0
ROT_AXIS_DIM = HEAD_DIM // 2
ROT_PAIR = ROT_AXIS_DIM // 2
POOL_WINDOWS = (2, 4, 8, 16)
POOL_WIDTH = 512
POOL_GROUP_DIM = 128
SGU_WIDTH = 512
N_SGU_HEADS = 4
SGU_CHUNK = 128
D_FF = 5632
N_MOD = 6
EPS = 1e-6
LOG2_E = 1.4426950408889634

SUBLANES = 8
BF16_SUBLANES = 16
LANES = 128
HALO = BF16_SUBLANES
MOD_ROWS = 16
IN_TILE = 512
FF_TILE = 512
FF_PARTS = 2
FF_DOWN_ROWS = 256
VMEM_LIMIT = 60 * 1024 * 1024
CAST_BLOCK_BYTES = 6 * 1024 * 1024


def _cparams(n_axes):
    return pltpu.CompilerParams(
        dimension_semantics=("arbitrary",) * n_axes, vmem_limit_bytes=VMEM_LIMIT)


def _rms(x, g):
    return x * lax.rsqrt(jnp.mean(x * x, axis=-1, keepdims=True) + EPS) * g


def _gelu(x):
    return 0.5 * x * (1.0 + lax.erf(x * 0.7071067811865476))


def _silu(x):
    return x * jax.nn.sigmoid(x)


def _norm_rows(dst_ref, x_ref, n_rows, gain, shift, *, dst_offset=0):
    rows = BF16_SUBLANES
    for r in range(0, n_rows, rows):
        xx = x_ref[r:r + rows, :]
        inv = lax.rsqrt(jnp.mean(xx * xx, axis=-1, keepdims=True) + EPS)
        dst_ref[dst_offset + r:dst_offset + r + rows, :] = (xx * inv * gain + shift).astype(BF16)


def _resident(shape, index_map):
    return pl.BlockSpec(shape, index_map, pipeline_mode=pl.Buffered(1))


def _mod_kernel(c_ref, w_ref, b_ref, o_ref):
    c = c_ref[...]
    s = _silu(c).astype(BF16)
    o_ref[...] = jnp.dot(s, w_ref[...].astype(BF16), preferred_element_type=F32) + b_ref[...]


def _modulation(cc, w_ada, b_ada):
    depth, d, n = w_ada.shape
    tn = 1024
    return pl.pallas_call(
        _mod_kernel,
        grid=(depth, n // tn),
        in_specs=[
            pl.BlockSpec((MOD_ROWS, d), lambda l, j: (0, 0)),
            pl.BlockSpec((None, d, tn), lambda l, j: (l, 0, j)),
            pl.BlockSpec((None, 1, tn), lambda l, j: (l, 0, j)),
        ],
        out_specs=pl.BlockSpec((None, MOD_ROWS, tn), lambda l, j: (l, 0, j)),
        out_shape=jax.ShapeDtypeStruct((depth, MOD_ROWS, n), F32),
        compiler_params=_cparams(2),
        name="modulation",
    )(cc, w_ada, b_ada.reshape(depth, 1, n))


def _cast_kernel(w_ref, o_ref, *, head_order_cols):
    w = w_ref[...]
    if head_order_cols:
        wh = w[:, :head_order_cols]
        quarter = (lax.broadcasted_iota(jnp.int32, wh.shape, 1) % HEAD_DIM) // ROT_PAIR
        wh = jnp.where(quarter == 1, pltpu.roll(wh, head_order_cols - ROT_PAIR, 1),
                       jnp.where(quarter == 2, pltpu.roll(wh, ROT_PAIR, 1), wh))
        w = jnp.concatenate([wh, w[:, head_order_cols:]], axis=1)
    o_ref[...] = w.astype(BF16)


def _to_bf16(w, head_order_cols=0):
    depth, rows, cols = w.shape
    total = depth * rows
    block_rows = CAST_BLOCK_BYTES // (4 * cols)
    block_rows = 1 << (block_rows.bit_length() - 1)
    assert block_rows % BF16_SUBLANES == 0 and total % block_rows == 0
    out = pl.pallas_call(
        functools.partial(_cast_kernel, head_order_cols=head_order_cols),
        grid=(total // block_rows,),
        in_specs=[pl.BlockSpec((block_rows, cols), lambda i: (i, 0))],
        out_specs=pl.BlockSpec((block_rows, cols), lambda i: (i, 0)),
        out_shape=jax.ShapeDtypeStruct((total, cols), BF16),
        compiler_params=_cparams(1),
        name="to_bf16",
    )(w.reshape(total, cols))
    return out.reshape(depth, rows, cols)


def _in_kernel(*refs, tm, tiles, rope):
    n_in = 9 if rope else 7
    x_ref, mod_ref, g1_ref, w_ref, qg_ref, kg_ref, sg_ref = refs[:7]
    cos_ref, sin_ref = refs[7:9] if rope else (None, None)
    out_refs = refs[n_in:-1]
    hs_ref = refs[-1]
    names = [nm for nm, ids in (("q", (0, 1)), ("kv", (2,)), ("pv", (3,)), ("u", (4,)), ("v", (5,)))
             if any(t in tiles for t in ids)]
    outs = dict(zip(names, out_refs))
    grp = 128

    gain = g1_ref[...] * (1.0 + mod_ref[1:2, :])
    _norm_rows(hs_ref, x_ref, tm, gain, mod_ref[0:1, :])
    hs = hs_ref[...]

    def project(i):
        return jnp.dot(hs, w_ref[:, i * IN_TILE:(i + 1) * IN_TILE], preferred_element_type=F32)

    def head(acc, r, h, g):
        xh = _rms(acc[r:r + grp, h * HEAD_DIM:(h + 1) * HEAD_DIM], g)
        if rope:
            xh = xh * cos_ref[r:r + grp, :] + pltpu.roll(xh, ROT_AXIS_DIM, 1) * sin_ref[r:r + grp, :]
        return xh.astype(BF16)

    def epilogue(t, acc):
        if t in (0, 1):
            qg = qg_ref[...] * (HEAD_DIM ** -0.5 * LOG2_E)
            for r in range(0, tm, grp):
                for h in range(IN_TILE // HEAD_DIM):
                    c0 = t * IN_TILE + h * HEAD_DIM
                    outs["q"][r:r + grp, c0:c0 + HEAD_DIM] = head(acc, r, h, qg)
        elif t == 2:
            kg = kg_ref[...]
            for r in range(0, tm, grp):
                for h in range(N_KV_HEADS):
                    outs["kv"][r:r + grp, h * HEAD_DIM:(h + 1) * HEAD_DIM] = head(acc, r, h, kg)
                outs["kv"][r:r + grp, KV_WIDTH:] = acc[r:r + grp, KV_WIDTH:].astype(BF16)
        elif t == 3:
            outs["pv"][...] = acc
        elif t == 4:
            for r in range(0, tm, grp):
                outs["u"][r:r + grp, :] = _gelu(acc[r:r + grp, :])
        else:
            sg = sg_ref[...]
            for r in range(0, tm, grp):
                outs["v"][r:r + grp, :] = _rms(_gelu(acc[r:r + grp, :]), sg).astype(BF16)

    nxt = project(0)
    for i, t in enumerate(tiles):
        acc = nxt
        if i + 1 < len(tiles):
            nxt = project(i + 1)
        epilogue(t, acc)


def _in_proj(x2, mod4, layer, mod_row, g1, w_in, qg, kg, sg, rope_tabs, *, seq, tm, tiles):
    rows, d = x2.shape
    rope = rope_tabs is not None
    assert rows % tm == 0 and (seq % tm == 0 or (mod_row is not None and not rope))
    tiles = tuple(tiles)
    assert tiles == tuple(range(tiles[0], tiles[0] + len(tiles))) and tiles[0] % len(tiles) == 0
    width = len(tiles) * IN_TILE
    wblk = tiles[0] // len(tiles)
    per_seq = max(seq // tm, 1)

    def mrow(m):
        return mod_row if mod_row is not None else m // per_seq

    in_specs = [
        pl.BlockSpec((tm, d), lambda m: (m, 0)),
        pl.BlockSpec((None, None, N_MOD, d), lambda m: (layer, mrow(m), 0, 0)),
        pl.BlockSpec((None, 1, d), lambda m: (layer, 0, 0)),
        _resident((None, d, width), lambda m: (layer, 0, wblk)),
        pl.BlockSpec((None, 1, HEAD_DIM), lambda m: (layer, 0, 0)),
        pl.BlockSpec((None, 1, HEAD_DIM), lambda m: (layer, 0, 0)),
        pl.BlockSpec((None, 1, SGU_WIDTH), lambda m: (layer, 0, 0)),
    ]
    args = [x2, mod4, g1, w_in, qg, kg, sg]
    if rope:
        in_specs += [pl.BlockSpec((tm, HEAD_DIM), lambda m: (m % per_seq, 0)) for _ in range(2)]
        args += list(rope_tabs)
    out_specs, out_shape = [], []

    def add(width, dtype):
        out_specs.append(pl.BlockSpec((tm, width), lambda m: (m, 0)))
        out_shape.append(jax.ShapeDtypeStruct((rows, width), dtype))

    if 0 in tiles or 1 in tiles:
        assert 0 in tiles and 1 in tiles
        add(ATTN_WIDTH, BF16)
    if 2 in tiles:
        add(IN_TILE, BF16)
    if 3 in tiles:
        add(POOL_WIDTH, F32)
    if 4 in tiles:
        add(SGU_WIDTH, F32)
    if 5 in tiles:
        add(SGU_WIDTH, BF16)
    return pl.pallas_call(
        functools.partial(_in_kernel, tm=tm, tiles=tiles, rope=rope),
        grid=(rows // tm,),
        in_specs=in_specs,
        out_specs=out_specs,
        out_shape=out_shape,
        scratch_shapes=[pltpu.VMEM((tm, d), BF16)],
        compiler_params=_cparams(1),
        name="in_proj",
    )(*args)


def _attn_unit(sink_ref, q_ref, kv_ref, kvc_ref, o_ref, qrows, kh, start, bias):
    rows = GQA_GROUP * BLOCK
    nband = 3 * BLOCK
    nt = (((1,), (1,)), ((), ()))
    q0 = kh * GQA_GROUP * HEAD_DIM
    kcol = slice(kh * HEAD_DIM, (kh + 1) * HEAD_DIM)
    vcol = slice(KV_WIDTH + kh * HEAD_DIM, KV_WIDTH + (kh + 1) * HEAD_DIM)
    qs = jnp.concatenate(
        [q_ref[qrows, q0 + g * HEAD_DIM:q0 + (g + 1) * HEAD_DIM] for g in range(GQA_GROUP)], axis=0)
    s_ctx = lax.dot_general(qs, kvc_ref[:, kcol], nt, preferred_element_type=F32)
    row = lax.broadcasted_iota(jnp.int32, (rows, 1), 0)
    sink = jnp.zeros((rows, 1), F32)
    for g in range(GQA_GROUP):
        sink = jnp.where(row // BLOCK == g, sink_ref[kh * GQA_GROUP + g] * LOG2_E, sink)
    m = jnp.maximum(jnp.max(s_ctx, axis=-1, keepdims=True), sink)
    if kv_ref is not None:
        s_band = lax.dot_general(qs, kv_ref[pl.ds(start, nband), kcol], nt,
                                 preferred_element_type=F32) + bias
        m = jnp.maximum(m, jnp.max(s_band, axis=-1, keepdims=True))
    p_ctx = jnp.exp2(s_ctx - m)
    den = jnp.sum(p_ctx, axis=-1, keepdims=True) + jnp.exp2(sink - m)
    o = jnp.dot(p_ctx.astype(BF16), kvc_ref[:, vcol], preferred_element_type=F32)
    if kv_ref is not None:
        p_band = jnp.exp2(s_band - m)
        den = den + jnp.sum(p_band, axis=-1, keepdims=True)
        o = o + jnp.dot(p_band.astype(BF16), kv_ref[pl.ds(start, nband), vcol],
                        preferred_element_type=F32)
    o = o * (1.0 / den)
    for g in range(GQA_GROUP):
        o_ref[qrows, q0 + g * HEAD_DIM:q0 + (g + 1) * HEAD_DIM] = (
            o[g * BLOCK:(g + 1) * BLOCK, :].astype(BF16))


def _attn_kernel(*refs, seq, band, qblocks):
    if band:
        sink_ref, q_ref, kv_ref, kvc_ref, o_ref = refs
    else:
        sink_ref, q_ref, kvc_ref, o_ref = refs
        kv_ref = None
    for j in range(qblocks):
        n = pl.program_id(1) * qblocks + j
        qrows = slice(j * BLOCK, (j + 1) * BLOCK)
        start = bias = None
        if band:
            nband = 3 * BLOCK
            start = pl.multiple_of(jnp.clip((n - 1) * BLOCK, 0, seq - nband), BLOCK)
            dist = (n * BLOCK - start) + (lax.broadcasted_iota(jnp.int32, (BLOCK, nband), 0)
                                          - lax.broadcasted_iota(jnp.int32, (BLOCK, nband), 1))
            bias = jnp.where(jnp.abs(dist) <= WINDOW, 0.0, -jnp.inf).astype(F32)
            bias = jnp.concatenate([bias] * GQA_GROUP, axis=0)
        for kh in range(N_KV_HEADS):
            _attn_unit(sink_ref, q_ref, kv_ref, kvc_ref, o_ref, qrows, kh, start, bias)


def _attention(sink, q, kv, kvc, *, batch, seq, ctx_len, band, qblocks):
    tq = qblocks * BLOCK
    assert seq % tq == 0
    steps = seq // tq
    in_specs = [
        pl.BlockSpec(memory_space=pltpu.SMEM),
        pl.BlockSpec((tq, ATTN_WIDTH), lambda b, n: (b * steps + n, 0)),
    ]
    args = [sink, q]
    if band:
        in_specs.append(pl.BlockSpec((seq, 2 * KV_WIDTH), lambda b, n: (b, 0)))
        args.append(kv)
    in_specs.append(pl.BlockSpec((ctx_len, 2 * KV_WIDTH), lambda b, n: (b, 0)))
    args.append(kvc)
    return pl.pallas_call(
        functools.partial(_attn_kernel, seq=seq, band=band, qblocks=qblocks),
        grid=(batch, steps),
        in_specs=in_specs,
        out_specs=pl.BlockSpec((tq, ATTN_WIDTH), lambda b, n: (b * steps + n, 0)),
        out_shape=jax.ShapeDtypeStruct((batch * seq, ATTN_WIDTH), BF16),
        compiler_params=_cparams(2),
        name="attention",
    )(*args)


def _out_kernel(attn_ref, pv_ref, pvp_ref, pvn_ref, u_ref, v_ref, x_ref, mod_ref, pw_ref, ps_ref,
                sw_ref, sbt_ref, wo_ref, o_ref, pbuf_ref, mix_ref, *, tm, seq):
    m = pl.program_id(0)
    pos0 = (m * tm) % seq
    o_ref[...] = jnp.dot(attn_ref[...], wo_ref[0:ATTN_WIDTH, :], preferred_element_type=F32)
    pbuf_ref[0:SUBLANES, :] = jnp.where(pos0 == 0, 0.0, pvp_ref[...])
    pbuf_ref[SUBLANES:SUBLANES + tm, :] = pv_ref[...]
    pbuf_ref[SUBLANES + tm:, :] = jnp.where(pos0 + tm == seq, 0.0, pvn_ref[...])
    pos = pos0 + lax.broadcasted_iota(jnp.int32, (tm, 1), 0)
    ext = tm + 2 * SUBLANES

    def ahead(a, k):
        return pltpu.roll(a, ext - k, 0)

    for gi, w in enumerate(POOL_WINDOWS):
        c0 = gi * POOL_GROUP_DIM
        cols = slice(c0, c0 + POOL_GROUP_DIM)
        run, span = pbuf_ref[:, cols], 1
        while 2 * span < w:
            run = run + ahead(run, span)
            span *= 2
        first = run[0:tm] if w // 2 == SUBLANES else ahead(run, SUBLANES - w // 2)[0:tm]
        acc = first + run[SUBLANES:SUBLANES + tm]
        lo = jnp.maximum(pos - w // 2, 0)
        hi = jnp.minimum(pos - w // 2 + w, seq)
        y = (acc * (1.0 / (hi - lo).astype(F32)) - pv_ref[:, cols]).astype(BF16)
        z = jnp.dot(y, pw_ref[gi], preferred_element_type=F32) * ps_ref[:, cols]
        mix_ref[:, cols] = z.astype(BF16)
    n_chunks = tm // SGU_CHUNK
    for h in range(N_SGU_HEADS):
        cols = slice(h * LANES, (h + 1) * LANES)
        vh = jnp.concatenate(
            [v_ref[ci * SGU_CHUNK:(ci + 1) * SGU_CHUNK, cols] for ci in range(n_chunks)], axis=1)
        r = jnp.dot(sw_ref[h], vh, preferred_element_type=F32) + sbt_ref[:, h:h + 1]
        for ci in range(n_chunks):
            rs = slice(ci * SGU_CHUNK, (ci + 1) * SGU_CHUNK)
            gated = u_ref[rs, cols] * r[:, ci * SGU_CHUNK:(ci + 1) * SGU_CHUNK]
            mix_ref[rs, POOL_WIDTH + h * LANES:POOL_WIDTH + (h + 1) * LANES] = gated.astype(BF16)
    res = o_ref[...] + jnp.dot(mix_ref[...], wo_ref[ATTN_WIDTH:, :], preferred_element_type=F32)
    o_ref[...] = x_ref[...] + mod_ref[2:3, :] * res


def _out_proj(attn, pv, u, v, x2, mod4, layer, mod_row, pool_w, pool_scale, sgu_w, sgu_bt, w_out,
              *, seq, tm):
    rows, d = x2.shape
    assert rows % tm == 0 and seq % tm == 0 and tm % SGU_CHUNK == 0
    per_seq = seq // tm
    hb = tm // SUBLANES
    last_hb = rows // SUBLANES - 1

    def mrow(m):
        return mod_row if mod_row is not None else m // per_seq

    def layer_block(a):
        shape = a.shape[1:]
        return pl.BlockSpec((None,) + shape, lambda m: (layer,) + (0,) * len(shape))

    return pl.pallas_call(
        functools.partial(_out_kernel, tm=tm, seq=seq),
        grid=(rows // tm,),
        in_specs=[
            pl.BlockSpec((tm, ATTN_WIDTH), lambda m: (m, 0)),
            pl.BlockSpec((tm, POOL_WIDTH), lambda m: (m, 0)),
            pl.BlockSpec((SUBLANES, POOL_WIDTH), lambda m: (jnp.maximum(m * hb - 1, 0), 0)),
            pl.BlockSpec((SUBLANES, POOL_WIDTH), lambda m: (jnp.minimum((m + 1) * hb, last_hb), 0)),
            pl.BlockSpec((tm, SGU_WIDTH), lambda m: (m, 0)),
            pl.BlockSpec((tm, SGU_WIDTH), lambda m: (m, 0)),
            pl.BlockSpec((tm, d), lambda m: (m, 0)),
            pl.BlockSpec((None, None, N_MOD, d), lambda m: (layer, mrow(m), 0, 0)),
            layer_block(pool_w),
            layer_block(pool_scale),
            layer_block(sgu_w),
            layer_block(sgu_bt),
            _resident((None,) + w_out.shape[1:], lambda m: (layer, 0, 0)),
        ],
        out_specs=pl.BlockSpec((tm, d), lambda m: (m, 0)),
        out_shape=jax.ShapeDtypeStruct((rows, d), F32),
        scratch_shapes=[pltpu.VMEM((tm + 2 * SUBLANES, POOL_WIDTH), F32),
                        pltpu.VMEM((tm, POOL_WIDTH + SGU_WIDTH), BF16)],
        compiler_params=_cparams(1),
        name="out_proj",
    )(attn, pv, pv, pv, u, v, x2, mod4, pool_w, pool_scale, sgu_w, sgu_bt, w_out)


def _ffn_kernel(x_hbm, xp_ref, xn_ref, mod_ref, g2_ref, wg_ref, wv_ref, cwg_ref, cwv_ref, cbg_ref,
                cbv_ref, wd_ref, fg_ref, o_ref, x_ref, x_sem, hs_ref, ag_ref, av_ref, act_ref,
                *, tm, seq, final):
    m = pl.program_id(0)
    f = pl.program_id(1)
    rows = BF16_SUBLANES
    grp = 128

    def x_copy(tile):
        return pltpu.make_async_copy(x_hbm.at[pl.ds(tile * tm, tm), :], x_ref, x_sem)

    @pl.when(f == 0)
    def _():
        @pl.when(m == 0)
        def _():
            x_copy(m).start()

        x_copy(m).wait()
        shift = mod_ref[3:4, :]
        gain = g2_ref[...] * (1.0 + mod_ref[4:5, :])
        _norm_rows(hs_ref, x_ref, tm, gain, shift, dst_offset=HALO)

        def halo(ref):
            xx = ref[...]
            inv = lax.rsqrt(jnp.mean(xx * xx, axis=-1, keepdims=True) + EPS)
            return xx * inv * gain + shift

        pos0 = (m * tm) % seq
        hn = jnp.where((pos0 + tm) % seq == 0, 0.0, halo(xn_ref))
        hp = jnp.where(pos0 == 0, 0.0, halo(xp_ref))
        pad = jnp.zeros_like(hp)
        hs_ref[0:HALO, :] = jnp.concatenate([pad, hp], axis=0).astype(BF16)
        hs_ref[HALO + tm:, :] = jnp.concatenate([hn, pad], axis=0).astype(BF16)
        o_ref[...] = x_ref[...]

    @pl.when((f == 1) & (m + 1 < pl.num_programs(0)))
    def _():
        x_copy(m + 1).start()

    hs = hs_ref[...]
    part = FF_TILE // FF_PARTS
    parts = [slice(p * part, (p + 1) * part) for p in range(FF_PARTS)]
    for cs in parts:
        ag_ref[:, cs] = jnp.dot(hs, wg_ref[:, cs], preferred_element_type=F32)
        av_ref[:, cs] = jnp.dot(hs, wv_ref[:, cs], preferred_element_type=F32)

    row_in_grp = lax.broadcasted_iota(jnp.int32, (grp, 1), 0)

    def conv(a_ref, cw_ref, cb_ref, r, cs):
        prev = a_ref[HALO - 1 + r:HALO - 1 + r + grp, cs]
        cur = a_ref[HALO + r:HALO + r + grp, cs]
        nxt = a_ref[HALO + 1 + r:HALO + 1 + r + grp, cs]
        if r > 0 and r % seq == 0:
            prev = jnp.where(row_in_grp == 0, 0.0, prev)
        if r + grp < tm and (r + grp) % seq == 0:
            nxt = jnp.where(row_in_grp == grp - 1, 0.0, nxt)
        return cb_ref[:, cs] + prev * cw_ref[0:1, cs] + cur * cw_ref[1:2, cs] + nxt * cw_ref[2:3, cs]

    for cs in parts:
        for r in range(0, tm, grp):
            act = _silu(conv(ag_ref, cwg_ref, cbg_ref, r, cs)) * conv(av_ref, cwv_ref, cbv_ref, r, cs)
            act_ref[r:r + grp, cs] = act.astype(BF16)
    gate2 = mod_ref[5:6, :]
    for r in range(0, tm, FF_DOWN_ROWS):
        rs = slice(r, min(r + FF_DOWN_ROWS, tm))
        o_ref[rs, :] += gate2 * jnp.dot(act_ref[rs, :], wd_ref[...], preferred_element_type=F32)

    if final:
        @pl.when(f == pl.num_programs(1) - 1)
        def _():
            fg = fg_ref[...]
            for r in range(0, tm, rows):
                o_ref[r:r + rows, :] = _rms(o_ref[r:r + rows, :], fg)


def _ffn(x2, mod4, layer, mod_row, g2, w_up, conv_w, conv_b, w_down, final_g, *, seq, tm, final):
    rows, d = x2.shape
    assert rows % tm == 0 and (seq % tm == 0 or tm % seq == 0) and seq % LANES == 0
    assert mod_row is not None or seq % tm == 0
    per_seq = max(seq // tm, 1)
    hb = tm // SUBLANES
    last_hb = rows // SUBLANES - 1
    nf = D_FF // FF_TILE

    def mrow(m):
        return mod_row if mod_row is not None else m // per_seq

    ext = tm + 2 * HALO
    assert nf >= 2
    return pl.pallas_call(
        functools.partial(_ffn_kernel, tm=tm, seq=seq, final=final),
        grid=(rows // tm, nf),
        in_specs=[
            pl.BlockSpec(memory_space=pl.ANY),
            pl.BlockSpec((SUBLANES, d), lambda m, f: (jnp.maximum(m * hb - 1, 0), 0)),
            pl.BlockSpec((SUBLANES, d), lambda m, f: (jnp.minimum((m + 1) * hb, last_hb), 0)),
            pl.BlockSpec((None, None, N_MOD, d), lambda m, f: (layer, mrow(m), 0, 0)),
            pl.BlockSpec((None, 1, d), lambda m, f: (layer, 0, 0)),
            pl.BlockSpec((None, d, FF_TILE), lambda m, f: (layer, 0, f)),
            pl.BlockSpec((None, d, FF_TILE), lambda m, f: (layer, 0, f + nf)),
            pl.BlockSpec((None, 3, FF_TILE), lambda m, f: (layer, 0, f)),
            pl.BlockSpec((None, 3, FF_TILE), lambda m, f: (layer, 0, f + nf)),
            pl.BlockSpec((None, 1, FF_TILE), lambda m, f: (layer, 0, f)),
            pl.BlockSpec((None, 1, FF_TILE), lambda m, f: (layer, 0, f + nf)),
            pl.BlockSpec((None, FF_TILE, d), lambda m, f: (layer, f, 0)),
            pl.BlockSpec((1, d), lambda m, f: (0, 0)),
        ],
        out_specs=pl.BlockSpec((tm, d), lambda m, f: (m, 0)),
        out_shape=jax.ShapeDtypeStruct((rows, d), F32),
        scratch_shapes=[pltpu.VMEM((tm, d), F32),
                        pltpu.SemaphoreType.DMA(()),
                        pltpu.VMEM((ext, d), BF16),
                        pltpu.VMEM((ext, FF_TILE), F32),
                        pltpu.VMEM((ext, FF_TILE), F32),
                        pltpu.VMEM((tm, FF_TILE), BF16)],
        compiler_params=_cparams(2),
        name="ffn",
    )(x2, x2, x2, mod4, g2, w_up, w_up, conv_w, conv_w, conv_b, conv_b, w_down, final_g)


def _head_lane_order(a, n_heads):
    lead = a.shape[:-1]
    a = a.reshape(lead + (n_heads, 2, 2, ROT_PAIR))
    return jnp.swapaxes(a, -2, -3).reshape(lead + (n_heads * HEAD_DIM,))


def _rope_tables(seq):
    rows = seq // GRID_W
    row_ids = jnp.repeat(jnp.arange(rows), GRID_W).astype(F32)
    col_ids = jnp.tile(jnp.arange(GRID_W), rows).astype(F32)
    inv = 1.0 / (ROPE_BASE ** (jnp.arange(0, ROT_AXIS_DIM, 2, dtype=F32) / ROT_AXIS_DIM))
    ang_r = row_ids[:, None] * inv
    ang_c = col_ids[:, None] * inv
    cr, sr, cc, sc = jnp.cos(ang_r), jnp.sin(ang_r), jnp.cos(ang_c), jnp.sin(ang_c)
    cos = jnp.concatenate([cr, cc, cr, cc], axis=-1)
    sin_signed = jnp.concatenate([-sr, -sc, sr, sc], axis=-1)
    return cos, sin_signed


def kernel(x, c, ctx, c_ctx, norm1_g, norm2_g, w_ada, b_ada, w_in, q_norm_g, k_norm_g, attn_sink,
           pool_w, pool_scale, sgu_norm_g, sgu_w, sgu_b, w_out, w_up, conv_w, conv_b, w_down,
           final_norm_g):
    batch, seq, d = x.shape
    ctx_len = ctx.shape[1]
    depth = w_ada.shape[0]
    assert d == D_MODEL and batch + 1 <= MOD_ROWS
    ctx_row = batch

    cc = jnp.zeros((MOD_ROWS, d), F32).at[:batch].set(c).at[ctx_row].set(c_ctx)
    mod4 = _modulation(cc, w_ada, b_ada).reshape(depth, MOD_ROWS, N_MOD, d)
    rope_tabs = _rope_tables(seq)

    w_in_b = _to_bf16(w_in, head_order_cols=ATTN_WIDTH + KV_WIDTH)
    w_out_b, w_up_b, w_down_b = (_to_bf16(w) for w in (w_out, w_up, w_down))
    pool_w_b, sgu_w_b = pool_w.astype(BF16), sgu_w.astype(BF16)
    g1, g2 = norm1_g[:, None], norm2_g[:, None]
    qg = _head_lane_order(q_norm_g, 1)[:, None]
    kg = _head_lane_order(k_norm_g, 1)[:, None]
    sg = sgu_norm_g[:, None]
    mixer = (pool_w_b, pool_scale[:, None], sgu_w_b, jnp.swapaxes(sgu_b, 1, 2), w_out_b)
    mlp = (g2, w_up_b, conv_w, conv_b[:, None], w_down_b, final_norm_g[None])

    x2 = x.reshape(batch * seq, d)
    xc2 = ctx.reshape(batch * ctx_len, d)
    lat_tm = 512
    in_tm = 1024
    ctx_tm = ctx_len
    ffn_tm = 1024
    ctx_ffn_tm = 1024
    attn_qblocks = 8
    for l in range(depth):
        last = l == depth - 1
        q, kv, pv, u, v = _in_proj(x2, mod4, l, None, g1, w_in_b, qg, kg, sg, rope_tabs,
                                   seq=seq, tm=in_tm, tiles=range(6))
        if last:
            (kvc,) = _in_proj(xc2, mod4, l, ctx_row, g1, w_in_b, qg, kg, sg, None,
                              seq=ctx_len, tm=in_tm, tiles=(2,))
        else:
            qc, kvc, pvc, uc, vc = _in_proj(xc2, mod4, l, ctx_row, g1, w_in_b, qg, kg, sg, None,
                                            seq=ctx_len, tm=in_tm, tiles=range(6))
        attn = _attention(attn_sink[l], q, kv, kvc, batch=batch, seq=seq, ctx_len=ctx_len, band=True,
                          qblocks=attn_qblocks)
        x2 = _out_proj(attn, pv, u, v, x2, mod4, l, None, *mixer, seq=seq, tm=lat_tm)
        if not last:
            attn_c = _attention(attn_sink[l], qc, None, kvc, batch=batch, seq=ctx_len,
                                ctx_len=ctx_len, band=False,
                                qblocks=min(attn_qblocks, ctx_len // BLOCK))
            xc2 = _out_proj(attn_c, pvc, uc, vc, xc2, mod4, l, ctx_row, *mixer, seq=ctx_len, tm=ctx_tm)
        x2 = _ffn(x2, mod4, l, None, *mlp, seq=seq, tm=ffn_tm, final=last)
        if not last:
            xc2 = _ffn(xc2, mod4, l, ctx_row, *mlp, seq=ctx_len, tm=ctx_ffn_tm, final=False)
    return x2.reshape(batch, seq, d)
```
